```python
import math
import jax, jax.numpy as jnp
from jax import lax
import numpy as np

D_MODEL = 2048
BATCH = 4
SEQ = 8192
DEPTH = 4

GRID_W = 64
CTX_LEN = 256
Q_BLOCK = 128
ROPE_THETA = 10000.0
EPS = 1e-6

MLA_HEADS = 8
MLA_DN = 64
MLA_DR = 32
MLA_DV = 64
MLA_Q_RANK = 384
MLA_KV_RANK = 256
DIFF_HEADS = 4
DIFF_D = 64
CONV_W = 512
CONV_K = 31
FNET_GROUPS = 4
FNET_GROUP_W = 128
N_BRANCH = 4
BRANCH_W = 512
N_EXPERTS = 16
N_GROUPS = 4
EXPERTS_PER_GROUP = N_EXPERTS // N_GROUPS
TOP_K = 2
D_FF = 1024

MLA_COLS = MLA_Q_RANK + MLA_KV_RANK + MLA_DR
DIFF_COLS = 3 * DIFF_HEADS * 2 * DIFF_D
CONV_COLS = 2 * CONV_W
FNET_COLS = FNET_GROUPS * FNET_GROUP_W
GATE_COLS = N_BRANCH * D_MODEL
OFF_DIFF = MLA_COLS
OFF_CONV = OFF_DIFF + DIFF_COLS
OFF_FNET = OFF_CONV + CONV_COLS
OFF_GATE = OFF_FNET + FNET_COLS
IN_COLS = OFF_GATE + GATE_COLS

DEEPNORM_ALPHA = (2.0 * DEPTH) ** 0.25
DEEPNORM_BETA = (8.0 * DEPTH) ** -0.25

kernel_name = 'hybrid_dit_mla_conformer_diffattn_fnet_grouped_moe'


def standardize(x):
    xf = x.astype(jnp.float32)
    mu = jnp.mean(xf, -1, keepdims=True)
    var = jnp.mean(jnp.square(xf - mu), -1, keepdims=True)
    return (xf - mu) * lax.rsqrt(var + EPS)


def layer_norm(x, g, b):
    return (standardize(x) * g + b).astype(x.dtype)


def rms_norm(x, g):
    xf = x.astype(jnp.float32)
    return (xf * lax.rsqrt(jnp.mean(jnp.square(xf), -1, keepdims=True) + EPS) * g).astype(x.dtype)


def axial_rope(rows, rot_dim):
    n = rows * GRID_W
    row = jnp.broadcast_to(jnp.arange(rows, dtype=jnp.float32)[:, None], (rows, GRID_W)).reshape(n)
    col = jnp.broadcast_to(jnp.arange(GRID_W, dtype=jnp.float32)[None, :], (rows, GRID_W)).reshape(n)
    quarter = rot_dim // 4
    inv_freq = ROPE_THETA ** (-jnp.arange(quarter, dtype=jnp.float32) / quarter)
    ar = row[:, None] * inv_freq
    ac = col[:, None] * inv_freq
    ang = jnp.concatenate([ar, ar, ac, ac], -1)
    return jnp.cos(ang), jnp.sin(ang)


def apply_rope(x, cos, sin):
    shape = (cos.shape[0],) + (1,) * (x.ndim - 3) + (cos.shape[1],)
    c = cos.reshape(shape)
    s = sin.reshape(shape)
    x1, x2, x3, x4 = jnp.split(x, 4, -1)
    rot = jnp.concatenate([-x2, x1, -x4, x3], -1)
    return (x * c + rot * s).astype(x.dtype)


def sweep_query_blocks(fn, q):
    B, S = q.shape[:2]
    nb = S // Q_BLOCK
    qb = jnp.moveaxis(q.reshape((B, nb, Q_BLOCK) + q.shape[2:]), 1, 0)
    out = lax.map(fn, qb)
    return jnp.moveaxis(out, 0, 1).reshape((B, S) + out.shape[3:])


def flat_heads(o):
    return o.reshape(o.shape[:2] + (-1,))


def softmax_attend(q, k, v):
    s = jnp.einsum('bqhd,bkhd->bhqk', q, k).astype(jnp.float32) * (q.shape[-1] ** -0.5)
    p = jax.nn.softmax(s, axis=-1).astype(v.dtype)
    return jnp.einsum('bhqk,bkhd->bqhd', p, v)


def diff_attend(q, k, v, lam):
    s = jnp.einsum('bqhmd,bkhmd->bhmqk', q, k).astype(jnp.float32) * (DIFF_D ** -0.5)
    p = jax.nn.softmax(s, axis=-1)
    pd = (p[:, :, 0] - lam * p[:, :, 1]).astype(v.dtype)
    return jnp.einsum('bhqk,bkhe->bqhe', pd, v)


def diff_finish(o, subln, lam_init):
    return flat_heads(rms_norm(o, subln) * (1.0 - lam_init))


def mla_qkv(cols, q_norm, w_uq, kv_norm, w_ukv, rope):
    B, S = cols.shape[:2]
    cq = cols[..., :MLA_Q_RANK]
    ckv = cols[..., MLA_Q_RANK:MLA_Q_RANK + MLA_KV_RANK]
    kr = cols[..., MLA_Q_RANK + MLA_KV_RANK:]
    q = (rms_norm(cq, q_norm) @ w_uq).reshape(B, S, MLA_HEADS, MLA_DN + MLA_DR)
    kv = (rms_norm(ckv, kv_norm) @ w_ukv).reshape(B, S, MLA_HEADS, MLA_DN + MLA_DV)
    q_nope, q_rope = q[..., :MLA_DN], q[..., MLA_DN:]
    k_nope, v = kv[..., :MLA_DN], kv[..., MLA_DN:]
    if rope is not None:
        q_rope = apply_rope(q_rope, *rope)
        kr = apply_rope(kr, *rope)
    k_rope = jnp.broadcast_to(kr[:, :, None, :], (B, S, MLA_HEADS, MLA_DR))
    return (jnp.concatenate([q_nope, q_rope], -1), jnp.concatenate([k_nope, k_rope], -1), v)


def diff_qkv(cols, rope):
    B, S = cols.shape[:2]
    w = DIFF_HEADS * 2 * DIFF_D
    q = cols[..., :w].reshape(B, S, DIFF_HEADS, 2, DIFF_D)
    k = cols[..., w:2 * w].reshape(B, S, DIFF_HEADS, 2, DIFF_D)
    v = cols[..., 2 * w:].reshape(B, S, DIFF_HEADS, 2 * DIFF_D)
    if rope is not None:
        q = apply_rope(q, *rope)
        k = apply_rope(k, *rope)
    return q, k, v


def conv_branch(cols, conv_w, conv_b, ln_g, ln_b):
    a, g = jnp.split(cols, 2, -1)
    z = a * jax.nn.sigmoid(g)
    z = lax.conv_general_dilated(z, conv_w.astype(z.dtype)[:, None, :], window_strides=(1,),
                                 padding=[(CONV_K // 2, CONV_K // 2)],
                                 dimension_numbers=('NWC', 'WIO', 'NWC'),
                                 feature_group_count=CONV_W) + conv_b
    return jax.nn.silu(layer_norm(z, ln_g, ln_b))


def fnet_branch(cols):
    B, S = cols.shape[:2]
    f = cols.reshape(B, S, FNET_GROUPS, FNET_GROUP_W).astype(jnp.float32)
    out = jnp.fft.fft2(f, axes=(1, 3), norm='ortho').real
    return out.astype(cols.dtype).reshape(B, S, FNET_COLS)


def merge(branches, gate_cols, w_branch, w_out):
    B, S = gate_cols.shape[:2]
    stacked = jnp.stack(branches, axis=2)
    proj = jnp.einsum('bsnw,nwd->bsnd', stacked, w_branch)
    gates = jax.nn.sigmoid(gate_cols.reshape(B, S, N_BRANCH, D_MODEL))
    return jnp.sum(gates * proj, axis=2) @ w_out


def token_mixer(h, hc, layer, update_ctx, rope_mla, rope_diff, w_in, mla_q_norm, mla_w_uq, mla_kv_norm,
                mla_w_ukv, diff_lq1, diff_lk1, diff_lq2, diff_lk2, diff_subln, conv_w, conv_b, conv_ln_g,
                conv_ln_b, w_branch, w_out):
    cols = h @ w_in
    cols_c = hc @ w_in
    q, k, v = mla_qkv(cols[..., :OFF_DIFF], mla_q_norm, mla_w_uq, mla_kv_norm, mla_w_ukv, rope_mla)
    qc, kc, vc = mla_qkv(cols_c[..., :OFF_DIFF], mla_q_norm, mla_w_uq, mla_kv_norm, mla_w_ukv, None)
    k_all = jnp.concatenate([k, kc], 1)
    v_all = jnp.concatenate([v, vc], 1)
    o_mla = flat_heads(sweep_query_blocks(lambda qb: softmax_attend(qb, k_all, v_all), q))
    lam_init = 0.8 - 0.6 * math.exp(-0.3 * layer)
    lam = (jnp.exp(jnp.sum(diff_lq1 * diff_lk1).astype(jnp.float32))
           - jnp.exp(jnp.sum(diff_lq2 * diff_lk2).astype(jnp.float32)) + lam_init)
    dq, dk, dv = diff_qkv(cols[..., OFF_DIFF:OFF_CONV], rope_diff)
    dqc, dkc, dvc = diff_qkv(cols_c[..., OFF_DIFF:OFF_CONV], None)
    dk_all = jnp.concatenate([dk, dkc], 1)
    dv_all = jnp.concatenate([dv, dvc], 1)
    o_diff = diff_finish(sweep_query_blocks(lambda qb: diff_attend(qb, dk_all, dv_all, lam), dq), diff_subln, lam_init)
    o_conv = conv_branch(cols[..., OFF_CONV:OFF_FNET], conv_w, conv_b, conv_ln_g, conv_ln_b)
    o_fnet = fnet_branch(cols[..., OFF_FNET:OFF_GATE])
    out = merge([o_mla, o_diff, o_conv, o_fnet], cols[..., OFF_GATE:], w_branch, w_out)
    if not update_ctx:
        return out, None
    oc_mla = flat_heads(softmax_attend(qc, kc, vc))
    oc_diff = diff_finish(diff_attend(dqc, dkc, dvc, lam), diff_subln, lam_init)
    oc_conv = conv_branch(cols_c[..., OFF_CONV:OFF_FNET], conv_w, conv_b, conv_ln_g, conv_ln_b)
    oc_fnet = fnet_branch(cols_c[..., OFF_FNET:OFF_GATE])
    out_c = merge([oc_mla, oc_diff, oc_conv, oc_fnet], cols_c[..., OFF_GATE:], w_branch, w_out)
    return out, out_c


def moe(tok, router_w, router_bias, w1, w3, w2):
    scores = jax.nn.sigmoid((tok @ router_w).astype(jnp.float32))
    sel = scores + router_bias.astype(jnp.float32)
    grp = sel.reshape(-1, N_GROUPS, EXPERTS_PER_GROUP)
    gscore = jnp.sum(lax.top_k(grp, TOP_K)[0], -1)
    best = jnp.argmax(gscore, -1)
    in_group = (jnp.arange(N_EXPERTS) // EXPERTS_PER_GROUP)[None, :] == best[:, None]
    _, idx = lax.top_k(jnp.where(in_group, sel, -jnp.inf), TOP_K)
    wsel = jnp.take_along_axis(scores, idx, -1)
    wsel = wsel / jnp.sum(wsel, -1, keepdims=True)
    combine = jnp.sum(jax.nn.one_hot(idx, N_EXPERTS, dtype=jnp.float32) * wsel[..., None], 1).astype(tok.dtype)
    out = jnp.zeros_like(tok)
    for e in range(N_EXPERTS):
        hid = jax.nn.silu(tok @ w1[e]) * (tok @ w3[e])
        out = out + combine[:, e:e + 1] * (hid @ w2[e])
    return out


def setup_inputs(seed: int = 0) -> dict:
    key = jax.random.key(seed)
    ks = jax.random.split(key, 32)
    f32 = jnp.float32

    def nrm(k, shape, scale):
        return jax.random.normal(k, shape, f32) * scale

    def gain(k, shape):
        return 1.0 + 0.02 * jax.random.normal(k, shape, f32)

    return {
        'x': nrm(ks[0], (BATCH, SEQ, D_MODEL), 1.0),
        'c': nrm(ks[1], (BATCH, D_MODEL), 1.0),
        'ctx': nrm(ks[2], (BATCH, CTX_LEN, D_MODEL), 1.0),
        'c_ctx': nrm(ks[3], (D_MODEL,), 1.0),
        'w_ada': nrm(ks[4], (DEPTH, D_MODEL, 6 * D_MODEL), D_MODEL ** -0.5),
        'b_ada': nrm(ks[5], (DEPTH, 6 * D_MODEL), 0.02),
        'w_in': nrm(ks[6], (DEPTH, D_MODEL, IN_COLS), D_MODEL ** -0.5),
        'mla_q_norm': gain(ks[7], (DEPTH, MLA_Q_RANK)),
        'mla_w_uq': nrm(ks[8], (DEPTH, MLA_Q_RANK, MLA_HEADS * (MLA_DN + MLA_DR)), MLA_Q_RANK ** -0.5),
        'mla_kv_norm': gain(ks[9], (DEPTH, MLA_KV_RANK)),
        'mla_w_ukv': nrm(ks[10], (DEPTH, MLA_KV_RANK, MLA_HEADS * (MLA_DN + MLA_DV)), MLA_KV_RANK ** -0.5),
        'diff_lq1': nrm(ks[11], (DEPTH, DIFF_D), 0.1),
        'diff_lk1': nrm(ks[12], (DEPTH, DIFF_D), 0.1),
        'diff_lq2': nrm(ks[13], (DEPTH, DIFF_D), 0.1),
        'diff_lk2': nrm(ks[14], (DEPTH, DIFF_D), 0.1),
        'diff_subln': gain(ks[15], (DEPTH, 2 * DIFF_D)),
        'conv_w': nrm(ks[16], (DEPTH, CONV_K, CONV_W), CONV_K ** -0.5),
        'conv_b': nrm(ks[17], (DEPTH, CONV_W), 0.02),
        'conv_ln_g': gain(ks[18], (DEPTH, CONV_W)),
        'conv_ln_b': nrm(ks[19], (DEPTH, CONV_W), 0.02),
        'w_branch': nrm(ks[20], (DEPTH, N_BRANCH, BRANCH_W, D_MODEL), BRANCH_W ** -0.5),
        'w_out': nrm(ks[21], (DEPTH, D_MODEL, D_MODEL), D_MODEL ** -0.5 * DEEPNORM_BETA),
        'ln1_g': gain(ks[22], (DEPTH, D_MODEL)),
        'ln1_b': nrm(ks[23], (DEPTH, D_MODEL), 0.02),
        'ln2_g': gain(ks[24], (DEPTH, D_MODEL)),
        'ln2_b': nrm(ks[25], (DEPTH, D_MODEL), 0.02),
        'router_w': nrm(ks[26], (D_MODEL, N_EXPERTS), D_MODEL ** -0.5),
        'router_bias': nrm(ks[27], (N_EXPERTS,), 0.01),
        'exp_w1': nrm(ks[28], (DEPTH, N_EXPERTS, D_MODEL, D_FF), D_MODEL ** -0.5),
        'exp_w3': nrm(ks[29], (DEPTH, N_EXPERTS, D_MODEL, D_FF), D_MODEL ** -0.5),
        'exp_w2': nrm(ks[30], (DEPTH, N_EXPERTS, D_FF, D_MODEL), D_FF ** -0.5 * DEEPNORM_BETA),
    }


def reference(x, c, ctx, c_ctx, w_ada, b_ada, w_in, mla_q_norm, mla_w_uq, mla_kv_norm, mla_w_ukv,
              diff_lq1, diff_lk1, diff_lq2, diff_lk2, diff_subln, conv_w, conv_b, conv_ln_g, conv_ln_b,
              w_branch, w_out, ln1_g, ln1_b, ln2_g, ln2_b, router_w, router_bias, exp_w1, exp_w3, exp_w2):
    B, S, D = x.shape
    Bc, C, _ = ctx.shape
    rows = S // GRID_W
    rope_mla = axial_rope(rows, MLA_DR)
    rope_diff = axial_rope(rows, DIFF_D)
    x = standardize(x).astype(x.dtype)
    ctx = standardize(ctx).astype(ctx.dtype)
    for l in range(DEPTH):
        last = l == DEPTH - 1
        mod = jax.nn.silu(c) @ w_ada[l] + b_ada[l]
        mod_c = jax.nn.silu(c_ctx) @ w_ada[l] + b_ada[l]
        sh1, sc1, g1, sh2, sc2, g2 = jnp.split(mod[:, None, :], 6, -1)
        sh1c, sc1c, g1c, sh2c, sc2c, g2c = jnp.split(mod_c, 6, -1)
        h = x * (1.0 + sc1) + sh1
        hc = ctx * (1.0 + sc1c) + sh1c
        mix, mix_c = token_mixer(h, hc, l, not last, rope_mla, rope_diff, w_in[l], mla_q_norm[l], mla_w_uq[l],
                                 mla_kv_norm[l], mla_w_ukv[l], diff_lq1[l], diff_lk1[l], diff_lq2[l],
                                 diff_lk2[l], diff_subln[l], conv_w[l], conv_b[l], conv_ln_g[l], conv_ln_b[l],
                                 w_branch[l], w_out[l])
        x = layer_norm(DEEPNORM_ALPHA * x + g1 * mix, ln1_g[l], ln1_b[l])
        h = (x * (1.0 + sc2) + sh2).reshape(B * S, D)
        if not last:
            ctx = layer_norm(DEEPNORM_ALPHA * ctx + g1c * mix_c, ln1_g[l], ln1_b[l])
            hc = (ctx * (1.0 + sc2c) + sh2c).reshape(Bc * C, D)
            f = moe(jnp.concatenate([h, hc], 0), router_w, router_bias, exp_w1[l], exp_w3[l], exp_w2[l])
            f_lat = f[:B * S].reshape(B, S, D)
            f_ctx = f[B * S:].reshape(Bc, C, D)
            ctx = layer_norm(DEEPNORM_ALPHA * ctx + g2c * f_ctx, ln2_g[l], ln2_b[l])
        else:
            f_lat = moe(h, router_w, router_bias, exp_w1[l], exp_w3[l], exp_w2[l]).reshape(B, S, D)
        x = layer_norm(DEEPNORM_ALPHA * x + g2 * f_lat, ln2_g[l], ln2_b[l])
    return x
```

```python
import functools
import math

import jax
import jax.numpy as jnp
from jax import lax
from jax.experimental import pallas as pl
from jax.experimental.pallas import tpu as pltpu

F32 = jnp.float32
BF16 = jnp.bfloat16

D_MODEL = 2048
GRID_W = 64
ROPE_THETA = 10000.0
EPS = 1e-6

MLA_HEADS = 8
MLA_DN = 64
MLA_DR = 32
MLA_DV = 64
MLA_Q_RANK = 384
MLA_KV_RANK = 256
DIFF_HEADS = 4
DIFF_D = 64
CONV_W = 512
CONV_K = 31
FNET_GROUPS = 4
FNET_GROUP_W = 128
N_BRANCH = 4
BRANCH_W = 512
N_EXPERTS = 16
N_GROUPS = 4
EXPERTS_PER_GROUP = N_EXPERTS // N_GROUPS
D_FF = 1024

MLA_COLS = MLA_Q_RANK + MLA_KV_RANK + MLA_DR
DIFF_COLS = 3 * DIFF_HEADS * 2 * DIFF_D
CONV_COLS = 2 * CONV_W
FNET_COLS = FNET_GROUPS * FNET_GROUP_W
GATE_COLS = N_BRANCH * D_MODEL
OFF_DIFF = MLA_COLS
OFF_CONV = OFF_DIFF + DIFF_COLS
OFF_FNET = OFF_CONV + CONV_COLS
OFF_GATE = OFF_FNET + FNET_COLS

LANES = 128
HEAD_PAD = 128
MLA_COLS_PAD = 768
CONV_HALO = 16
VMEM_LIMIT = 56 * 1024 * 1024
LOG2E = math.log2(math.e)

ROW_TILE = 512


def _params(*sem):
    return pltpu.CompilerParams(dimension_semantics=sem, vmem_limit_bytes=VMEM_LIMIT)


def _mm_kernel(a_ref, w_ref, o_ref, acc_ref, *, nk):
    prod = jnp.dot(a_ref[...], w_ref[...], preferred_element_type=F32)
    if nk == 1:
        o_ref[...] = prod.astype(o_ref.dtype)
        return
    k = pl.program_id(3)

    @pl.when(k == 0)
    def _():
        acc_ref[...] = prod

    @pl.when(k > 0)
    def _():
        acc_ref[...] += prod

    @pl.when(k == nk - 1)
    def _():
        o_ref[...] = acc_ref[...].astype(o_ref.dtype)


def _matmul(a, w, *, out_shape, out_dtype, tm, tn, tk, grid, a_map, w_map, o_map, name):
    nk = grid[3]
    return pl.pallas_call(
        functools.partial(_mm_kernel, nk=nk),
        grid=grid,
        in_specs=[pl.BlockSpec((tm, tk), a_map), pl.BlockSpec((tk, tn), w_map)],
        out_specs=pl.BlockSpec((tm, tn), o_map),
        out_shape=jax.ShapeDtypeStruct(out_shape, out_dtype),
        scratch_shapes=[pltpu.VMEM((tm, tn), F32)],
        compiler_params=_params("parallel", "parallel", "parallel", "arbitrary"),
        name=name,
    )(a, w)


def _project(h, w, out_dtype, name, tn):
    R, K = h.shape
    N = w.shape[1]
    tm = ROW_TILE
    return _matmul(h, w, out_shape=(R, N), out_dtype=out_dtype, tm=tm, tn=tn, tk=K,
                   grid=(R // tm, N // tn, 1, 1),
                   a_map=lambda i, j, u, k: (i, 0), w_map=lambda i, j, u, k: (0, j),
                   o_map=lambda i, j, u, k: (i, j), name=name)


def _ada_kernel(c_ref, w_ref, b_ref, o_ref):
    c = c_ref[...]
    a = (c * jax.nn.sigmoid(c)).astype(BF16)
    o_ref[...] = jnp.dot(a, w_ref[...].astype(BF16), preferred_element_type=F32) + b_ref[...]


def _ada(cond, w_ada, b_ada):
    L, D, N6 = w_ada.shape
    rows = cond.shape[0]
    tn = 1024
    return pl.pallas_call(
        _ada_kernel,
        grid=(L, N6 // tn),
        in_specs=[pl.BlockSpec((rows, D), lambda l, j: (0, 0)),
                  pl.BlockSpec((None, D, tn), lambda l, j: (l, 0, j)),
                  pl.BlockSpec((None, 1, tn), lambda l, j: (l, 0, j))],
        out_specs=pl.BlockSpec((None, rows, tn), lambda l, j: (l, 0, j)),
        out_shape=jax.ShapeDtypeStruct((L, rows, N6), F32),
        compiler_params=_params("parallel", "parallel"),
        name="ada_mod",
    )(cond, w_ada, b_ada.reshape(L, 1, N6))


def _route(logits_t, bias_t):
    scores = jax.nn.sigmoid(logits_t)
    sel = scores + bias_t
    sel_r = [sel[e:e + 1, :] for e in range(N_EXPERTS)]
    sc_r = [scores[e:e + 1, :] for e in range(N_EXPERTS)]
    per = EXPERTS_PER_GROUP
    gscore = []
    for g in range(N_GROUPS):
        r = sel_r[g * per:(g + 1) * per]
        best = None
        for a in range(per):
            for b in range(a + 1, per):
                s = r[a] + r[b]
                best = s if best is None else jnp.maximum(best, s)
        gscore.append(best)
    best_g = jnp.zeros_like(gscore[0], dtype=jnp.int32)
    best_v = gscore[0]
    for g in range(1, N_GROUPS):
        better = gscore[g] > best_v
        best_g = jnp.where(better, g, best_g)
        best_v = jnp.where(better, gscore[g], best_v)
    v = []
    s = []
    for j in range(per):
        vj = sel_r[j]
        sj = sc_r[j]
        for g in range(1, N_GROUPS):
            vj = jnp.where(best_g == g, sel_r[g * per + j], vj)
            sj = jnp.where(best_g == g, sc_r[g * per + j], sj)
        v.append(vj)
        s.append(sj)
    chosen = []
    for j in range(per):
        rank = jnp.zeros_like(best_g)
        for i in range(per):
            if i == j:
                continue
            ahead = (v[i] > v[j]) | ((v[i] == v[j]) & (i < j))
            rank = rank + ahead.astype(jnp.int32)
        chosen.append(rank < 2)
    total = jnp.zeros_like(s[0])
    for j in range(per):
        total = total + jnp.where(chosen[j], s[j], 0.0)
    rows = []
    for e in range(N_EXPERTS):
        g, j = divmod(e, per)
        rows.append(jnp.where((best_g == g) & chosen[j], s[j] / total, 0.0))
    return jnp.concatenate(rows, axis=0)


def _norm_kernel(*refs, first, emit_h, router, alpha):
    refs = list(refs)
    x_ref = refs.pop(0)
    if not first:
        m_ref, g_ref, lng_ref, lnb_ref = refs[:4]
        refs = refs[4:]
    if emit_h:
        sc_ref, sh_ref = refs[:2]
        refs = refs[2:]
    if router:
        rwh_ref, rwl_ref, rb_ref = refs[:3]
        refs = refs[3:]
    xo_ref = refs.pop(0)
    x = x_ref[...]
    y = x if first else alpha * x + g_ref[...] * m_ref[...]
    mu = jnp.mean(y, axis=-1, keepdims=True)
    d = y - mu
    var = jnp.mean(d * d, axis=-1, keepdims=True)
    xn = d * lax.rsqrt(var + EPS)
    if not first:
        xn = xn * lng_ref[...] + lnb_ref[...]
    xo_ref[...] = xn
    if not emit_h:
        return
    h_ref = refs.pop(0)
    h = xn * (1.0 + sc_ref[...]) + sh_ref[...]
    h_hi = h.astype(BF16)
    h_ref[...] = h_hi
    if router:
        comb_ref = refs.pop(0)
        h_lo = (h - h_hi.astype(F32)).astype(BF16)
        nt = (((1,), (1,)), ((), ()))
        logits_t = (lax.dot_general(rwh_ref[...], h_hi, nt, preferred_element_type=F32)
                    + lax.dot_general(rwh_ref[...], h_lo, nt, preferred_element_type=F32)
                    + lax.dot_general(rwl_ref[...], h_hi, nt, preferred_element_type=F32))
        comb_ref[...] = _route(logits_t, rb_ref[...])


def _norm(x, *, seg, alpha, first=False, m=None, mod_g=None, k_gate=None, ln_g=None, ln_b=None,
          mod_h=None, k_scale=None, k_shift=None, router=None, name):
    R, D = x.shape
    tm = ROW_TILE
    emit_h = mod_h is not None
    row = pl.BlockSpec((tm, D), lambda i: (i, 0))

    def mod_spec(k):
        return pl.BlockSpec((None, 1, D), lambda i: (seg(i) * 6 + k, 0, 0))

    vec = pl.BlockSpec((1, D), lambda i: (0, 0))
    args, specs = [x], [row]
    if not first:
        args += [m, mod_g, ln_g.reshape(1, D), ln_b.reshape(1, D)]
        specs += [row, mod_spec(k_gate), vec, vec]
    if emit_h:
        args += [mod_h, mod_h]
        specs += [mod_spec(k_scale), mod_spec(k_shift)]
    if router is not None:
        rwh, rwl, rb = router
        args += [rwh, rwl, rb]
        specs += [pl.BlockSpec((N_EXPERTS, D), lambda i: (0, 0))] * 2 + [pl.BlockSpec((N_EXPERTS, 1), lambda i: (0, 0))]
    out_shape = [jax.ShapeDtypeStruct((R, D), F32)]
    out_specs = [row]
    if emit_h:
        out_shape.append(jax.ShapeDtypeStruct((R, D), BF16))
        out_specs.append(row)
    if router is not None:
        out_shape.append(jax.ShapeDtypeStruct((N_EXPERTS, R), F32))
        out_specs.append(pl.BlockSpec((N_EXPERTS, tm), lambda i: (0, i)))
    return pl.pallas_call(
        functools.partial(_norm_kernel, first=first, emit_h=emit_h, router=router is not None, alpha=alpha),
        grid=(R // tm,),
        in_specs=specs,
        out_specs=out_specs,
        out_shape=out_shape,
        compiler_params=_params("parallel"),
        name=name,
    )(*args)


def _rope_tables(rows, rot_dim, lane0, chunk, reps, ident_rows):
    n = rows * GRID_W
    r = jnp.broadcast_to(jnp.arange(rows, dtype=F32)[:, None], (rows, GRID_W)).reshape(n)
    c = jnp.broadcast_to(jnp.arange(GRID_W, dtype=F32)[None, :], (rows, GRID_W)).reshape(n)
    quarter = rot_dim // 4
    inv_freq = ROPE_THETA ** (-jnp.arange(quarter, dtype=F32) / quarter)
    ar = r[:, None] * inv_freq
    ac = c[:, None] * inv_freq
    ang = jnp.concatenate([ar, ar, ac, ac], -1)
    sign = jnp.concatenate([-jnp.ones((quarter,), F32), jnp.ones((quarter,), F32)] * 2)
    cos = jnp.ones((n, chunk), F32).at[:, lane0:lane0 + rot_dim].set(jnp.cos(ang))
    sin = jnp.zeros((n, chunk), F32).at[:, lane0:lane0 + rot_dim].set(jnp.sin(ang) * sign)
    cos = jnp.concatenate([cos, jnp.ones((ident_rows, chunk), F32)], 0)
    sin = jnp.concatenate([sin, jnp.zeros((ident_rows, chunk), F32)], 0)
    return jnp.tile(cos, (1, reps)), jnp.tile(sin, (1, reps))


def _rope(x, cos, sin_signed, quarter, lane0):
    w = x.shape[-1]
    lane = lax.broadcasted_iota(jnp.int32, x.shape, 1)
    even = (((lane - lane0) // quarter) % 2) == 0
    partner = jnp.where(even, pltpu.roll(x, w - quarter, 1), pltpu.roll(x, quarter, 1))
    return x * cos + partner * sin_signed


def _rms(x, g):
    return x * lax.rsqrt(jnp.mean(x * x, axis=-1, keepdims=True) + EPS) * g


def _mla_prep_kernel(cm_ref, qn_ref, kvn_ref, wq_ref, wk_ref, wv_ref, cq_ref, sq_ref, ck_ref, sk_ref,
                     q_ref, k_ref, v_ref):
    cm = cm_ref[...]
    cq = cm[:, :MLA_Q_RANK]
    ckv = cm[:, MLA_Q_RANK:MLA_Q_RANK + MLA_KV_RANK]
    kr = cm[:, MLA_Q_RANK + MLA_KV_RANK:]
    qn = _rms(cq, qn_ref[...]).astype(BF16)
    kvn = _rms(ckv, kvn_ref[...]).astype(BF16)
    q = jnp.dot(qn, wq_ref[...], preferred_element_type=F32)
    q = _rope(q, cq_ref[...], sq_ref[...], MLA_DR // 4, MLA_DN)
    scale = (MLA_DN + MLA_DR) ** -0.5 * LOG2E
    q_ref[...] = (q * scale).astype(BF16)
    kr = _rope(kr, ck_ref[...], sk_ref[...], MLA_DR // 4, 0)
    kr = pltpu.roll(kr, MLA_DN, 1)
    k = jnp.dot(kvn, wk_ref[...], preferred_element_type=F32)
    k_ref[...] = (k + jnp.concatenate([kr] * MLA_HEADS, axis=1)).astype(BF16)
    v = jnp.dot(kvn, wv_ref[...], preferred_element_type=F32)
    lane = lax.broadcasted_iota(jnp.int32, v.shape, 1)
    v_ref[...] = jnp.where(lane % HEAD_PAD == MLA_DV, 1.0, v).astype(BF16)


def _mla_prep(cm, q_norm, kv_norm, wq, wk, wv, tabs_q, tabs_k, pos_block):
    R = cm.shape[0]
    tm = ROW_TILE
    W = MLA_HEADS * HEAD_PAD
    row = lambda w: pl.BlockSpec((tm, w), lambda i: (i, 0))
    full = lambda a: pl.BlockSpec(a.shape, lambda i: (0, 0))
    tab = lambda w: pl.BlockSpec((tm, w), lambda i: (pos_block(i), 0))
    outs = pl.pallas_call(
        _mla_prep_kernel,
        grid=(R // tm,),
        in_specs=[row(MLA_COLS_PAD), full(q_norm), full(kv_norm), full(wq), full(wk), full(wv),
                  tab(W), tab(W), tab(LANES), tab(LANES)],
        out_specs=[row(W)] * 3,
        out_shape=[jax.ShapeDtypeStruct((R, W), BF16)] * 3,
        compiler_params=_params("parallel"),
        name="mla_prep",
    )(cm, q_norm, kv_norm, wq, wk, wv, tabs_q[0], tabs_q[1], tabs_k[0], tabs_k[1])
    return outs


def _diff_prep_kernel(cd_ref, cos_ref, sin_ref, q_ref, k_ref, v_ref):
    w = DIFF_HEADS * 2 * DIFF_D
    cd = cd_ref[...].astype(F32)
    cos = cos_ref[...]
    sin = sin_ref[...]
    q = _rope(cd[:, :w], cos, sin, DIFF_D // 4, 0)
    q_ref[...] = (q * (DIFF_D ** -0.5 * LOG2E)).astype(BF16)
    k_ref[...] = _rope(cd[:, w:2 * w], cos, sin, DIFF_D // 4, 0).astype(BF16)
    v = cd_ref[:, 2 * w:]
    lane = lax.broadcasted_iota(jnp.int32, (v.shape[0], LANES), 1)
    ones = jnp.where(lane == 0, 1.0, 0.0).astype(BF16)
    parts = []
    for h in range(DIFF_HEADS):
        parts += [v[:, h * 2 * DIFF_D:(h + 1) * 2 * DIFF_D], ones]
    v_ref[...] = jnp.concatenate(parts, axis=1)


def _diff_prep(cd, tabs, pos_block):
    R = cd.shape[0]
    tm = ROW_TILE
    w = DIFF_HEADS * 2 * DIFF_D
    row = lambda c: pl.BlockSpec((tm, c), lambda i: (i, 0))
    tab = pl.BlockSpec((tm, w), lambda i: (pos_block(i), 0))
    return pl.pallas_call(
        _diff_prep_kernel,
        grid=(R // tm,),
        in_specs=[row(DIFF_COLS), tab, tab],
        out_specs=[row(w), row(w), row(2 * w)],
        out_shape=[jax.ShapeDtypeStruct((R, w), BF16), jax.ShapeDtypeStruct((R, w), BF16),
                   jax.ShapeDtypeStruct((R, 2 * w), BF16)],
        compiler_params=_params("parallel"),
        name="diff_prep",
    )(cd, tabs[0], tabs[1])


def _flash(q, sources, vw):
    tq = q.shape[0]
    m = jnp.full((tq, 1), -jnp.inf, F32)
    acc = jnp.zeros((tq, vw), F32)
    for k_ref, v_ref, kl0, vl0, length, tk in sources:
        def step(c, carry, k_ref=k_ref, v_ref=v_ref, kl0=kl0, vl0=vl0, tk=tk):
            m, acc = carry
            r0 = pl.multiple_of(c * tk, tk)
            k = k_ref[pl.ds(r0, tk), kl0:kl0 + HEAD_PAD]
            v = v_ref[pl.ds(r0, tk), vl0:vl0 + vw]
            s = lax.dot_general(q, k, (((1,), (1,)), ((), ())), preferred_element_type=F32)
            m_new = jnp.maximum(m, jnp.max(s, axis=1, keepdims=True))
            p = jnp.exp2(s - m_new)
            a = jnp.exp2(m - m_new)
            acc = a * acc + jnp.dot(p.astype(BF16), v, preferred_element_type=F32)
            return m_new, acc
        n = length // tk
        if n == 1:
            m, acc = step(0, (m, acc))
        else:
            m, acc = lax.fori_loop(0, n, step, (m, acc))
    return acc


def _mla_attn_kernel(*refs, n_src, lens, tk):
    q_ref = refs[0]
    kv = refs[1:1 + 2 * n_src]
    o_ref = refs[1 + 2 * n_src]
    outs = []
    for hh in range(2):
        q = q_ref[:, hh * HEAD_PAD:(hh + 1) * HEAD_PAD]
        sources = [(kv[2 * s], kv[2 * s + 1], hh * HEAD_PAD, hh * HEAD_PAD, lens[s], min(tk, lens[s]))
                   for s in range(n_src)]
        acc = _flash(q, sources, HEAD_PAD)
        outs.append(acc[:, :MLA_DV] / acc[:, MLA_DV:MLA_DV + 1])
    o_ref[...] = jnp.concatenate(outs, axis=1).astype(BF16)


def _diff_attn_kernel(*refs, n_src, lens, tk, lam_init):
    q_ref, ll_ref, sub_ref = refs[:3]
    kv = refs[3:3 + 2 * n_src]
    o_ref = refs[3 + 2 * n_src]
    ll = ll_ref[...]
    lam = (jnp.exp(jnp.sum(ll[0:1, :] * ll[1:2, :], axis=1, keepdims=True))
           - jnp.exp(jnp.sum(ll[2:3, :] * ll[3:4, :], axis=1, keepdims=True)) + lam_init)
    q = q_ref[...]
    lane = lax.broadcasted_iota(jnp.int32, q.shape, 1)
    vw = 2 * HEAD_PAD
    dv = 2 * DIFF_D
    o = []
    for mp in range(2):
        qm = jnp.where((lane // DIFF_D) == mp, q, jnp.zeros_like(q))
        sources = [(kv[2 * s], kv[2 * s + 1], 0, 0, lens[s], min(tk, lens[s])) for s in range(n_src)]
        acc = _flash(qm, sources, vw)
        o.append(acc[:, :dv] / acc[:, dv:dv + 1])
    od = o[0] - lam * o[1]
    o_ref[...] = (_rms(od, sub_ref[...]) * (1.0 - lam_init)).astype(BF16)


def _attention(kind, q, k, v, *, B, S, C, ctx_queries, extra=None, lam_init=None):
    N = B * S
    tk = 512
    if kind == "mla":
        steps, qw, kw, vw, ow = MLA_HEADS // 2, 2 * HEAD_PAD, 2 * HEAD_PAD, 2 * HEAD_PAD, 2 * MLA_DV
    else:
        steps, qw, kw, vw, ow = DIFF_HEADS, HEAD_PAD, HEAD_PAD, 2 * HEAD_PAD, 2 * DIFF_D
    cblk = N // C
    if ctx_queries:
        tq, nq, rows_out = C, 1, B * C
        q_spec = pl.BlockSpec((tq, qw), lambda b, h, i: (cblk + b, h))
        kv_specs = [pl.BlockSpec((C, kw), lambda b, h, i: (cblk + b, h)),
                    pl.BlockSpec((C, vw), lambda b, h, i: (cblk + b, h))]
        lens = (C,)
    else:
        tq = 512
        nq, rows_out = S // tq, N
        q_spec = pl.BlockSpec((tq, qw), lambda b, h, i: (b * nq + i, h))
        kv_specs = [pl.BlockSpec((S, kw), lambda b, h, i: (b, h)),
                    pl.BlockSpec((S, vw), lambda b, h, i: (b, h)),
                    pl.BlockSpec((C, kw), lambda b, h, i: (cblk + b, h)),
                    pl.BlockSpec((C, vw), lambda b, h, i: (cblk + b, h))]
        lens = (S, C)
    n_src = len(lens)
    kv_args = [k, v] * n_src
    o_spec = pl.BlockSpec((tq, ow), lambda b, h, i: (b * nq + i, h))
    if kind == "mla":
        body = functools.partial(_mla_attn_kernel, n_src=n_src, lens=lens, tk=tk)
        args, specs = [q], [q_spec]
    else:
        ll, sub = extra
        body = functools.partial(_diff_attn_kernel, n_src=n_src, lens=lens, tk=tk, lam_init=lam_init)
        args = [q, ll, sub]
        specs = [q_spec, pl.BlockSpec(ll.shape, lambda b, h, i: (0, 0)), pl.BlockSpec(sub.shape, lambda b, h, i: (0, 0))]
    return pl.pallas_call(
        body,
        grid=(B, steps, nq),
        in_specs=specs + kv_specs,
        out_specs=o_spec,
        out_shape=jax.ShapeDtypeStruct((rows_out, steps * ow), BF16),
        compiler_params=_params("parallel", "parallel", "parallel"),
        name=f"{kind}_attn_{'ctx' if ctx_queries else 'lat'}",
    )(*args, *kv_args)


def _conv_kernel(main_ref, prev_ref, next_ref, w_ref, b_ref, g_ref, bb_ref, o_ref, z_ref, *, ts, nt):
    i = pl.program_id(1)

    def glu(blk):
        a = blk[:, :CONV_W].astype(F32)
        g = blk[:, CONV_W:].astype(F32)
        return a * jax.nn.sigmoid(g)

    H = CONV_HALO
    z_ref[0:H, :] = glu(prev_ref[...]) * (i > 0).astype(F32)
    z_ref[H:H + ts, :] = glu(main_ref[...])
    z_ref[H + ts:2 * H + ts, :] = glu(next_ref[...]) * (i < nt - 1).astype(F32)
    rc = 32
    first = H - CONV_K // 2
    for r in range(ts // rc):
        acc = jnp.broadcast_to(b_ref[...], (rc, CONV_W))
        for k in range(CONV_K):
            acc = acc + w_ref[k:k + 1, :] * z_ref[r * rc + k + first:r * rc + k + first + rc, :]
        mu = jnp.mean(acc, axis=-1, keepdims=True)
        d = acc - mu
        var = jnp.mean(d * d, axis=-1, keepdims=True)
        y = d * lax.rsqrt(var + EPS) * g_ref[...] + bb_ref[...]
        o_ref[r * rc:(r + 1) * rc, :] = (y * jax.nn.sigmoid(y)).astype(BF16)


def _conv(cols, row_off, nb, L, conv_w, conv_b, ln_g, ln_b):
    R = cols.shape[0]
    ts = min(ROW_TILE, L)
    nt = L // ts
    H = CONV_HALO
    hb = ts // H
    base = lambda b, i: (row_off + b * L + i * ts) // H
    vec = lambda a: pl.BlockSpec(a.shape, lambda b, i: (0, 0))
    cb, g, bb = conv_b.reshape(1, CONV_W), ln_g.reshape(1, CONV_W), ln_b.reshape(1, CONV_W)
    return pl.pallas_call(
        functools.partial(_conv_kernel, ts=ts, nt=nt),
        grid=(nb, nt),
        in_specs=[pl.BlockSpec((ts, CONV_COLS), lambda b, i: ((row_off + b * L) // ts + i, 0)),
                  pl.BlockSpec((H, CONV_COLS), lambda b, i: (jnp.maximum(base(b, i) - 1, 0), 0)),
                  pl.BlockSpec((H, CONV_COLS), lambda b, i: (jnp.minimum(base(b, i) + hb, R // H - 1), 0)),
                  vec(conv_w), vec(cb), vec(g), vec(bb)],
        out_specs=pl.BlockSpec((ts, CONV_W), lambda b, i: (b * nt + i, 0)),
        out_shape=jax.ShapeDtypeStruct((nb * L, CONV_W), BF16),
        scratch_shapes=[pltpu.VMEM((ts + 2 * H, CONV_W), F32)],
        compiler_params=_params("parallel", "parallel"),
        name=f"conv_{L}",
    )(cols, cols, cols, conv_w, cb, g, bb)


def _dft_channel_matrix():
    j = jnp.arange(FNET_GROUP_W, dtype=jnp.int32)
    ang = (2.0 * math.pi / FNET_GROUP_W) * ((j[:, None] * j[None, :]) % FNET_GROUP_W).astype(F32)
    eye = jnp.eye(FNET_GROUPS, dtype=F32)
    scale = FNET_GROUP_W ** -0.5
    return jnp.concatenate([jnp.kron(eye, jnp.cos(ang)), jnp.kron(eye, jnp.sin(ang))], axis=1) * scale


def _dft_position_matrix(L):
    n = jnp.arange(L, dtype=jnp.int32)
    ang = (2.0 * math.pi / L) * ((n[:, None] * n[None, :]) % L).astype(F32)
    return jnp.concatenate([jnp.cos(ang), -jnp.sin(ang)], axis=1) * (L ** -0.5)


def _fnet_positions(y, a, row_off, nb, L):
    tm = min(ROW_TILE, L)
    tk = min(2048, L)
    kb = L // tk
    off = row_off // tk
    return _matmul(a, y, out_shape=(nb * L, FNET_COLS), out_dtype=BF16, tm=tm, tn=FNET_COLS, tk=tk,
                   grid=(nb, L // tm, 1, 2 * kb),
                   a_map=lambda b, i, u, k: (i, k),
                   w_map=lambda b, i, u, k: (off + b * kb + k % kb, k // kb),
                   o_map=lambda b, i, u, k: (b * (L // tm) + i, 0),
                   name=f"fnet_pos_{L}")


def _merge_kernel(*refs):
    br = refs[0:4]
    gt = refs[4:8]
    wb = refs[8:12]
    wo_ref, o_ref, acc_ref = refs[12:15]
    j = pl.program_id(1)
    mj = None
    for n in range(N_BRANCH):
        proj = jnp.dot(br[n][...], wb[n][...], preferred_element_type=F32)
        t = jax.nn.sigmoid(gt[n][...].astype(F32)) * proj
        mj = t if mj is None else mj + t
    part = jnp.dot(mj.astype(BF16), wo_ref[...], preferred_element_type=F32)

    @pl.when(j == 0)
    def _():
        acc_ref[...] = part

    @pl.when(j > 0)
    def _():
        acc_ref[...] += part

    @pl.when(j == pl.num_programs(1) - 1)
    def _():
        o_ref[...] = acc_ref[...]


def _merge(branches, gates, w_branch, w_out):
    R = gates.shape[0]
    D = D_MODEL
    tm = ROW_TILE
    tj = BRANCH_W
    nj = D // tj
    specs = [pl.BlockSpec((tm, BRANCH_W), lambda i, j: (i, 0))] * 4
    specs += [pl.BlockSpec((tm, tj), functools.partial(lambda i, j, n: (i, n * nj + j), n=n)) for n in range(4)]
    specs += [pl.BlockSpec((None, BRANCH_W, tj), functools.partial(lambda i, j, n: (n, 0, j), n=n)) for n in range(4)]
    specs += [pl.BlockSpec((tj, D), lambda i, j: (j, 0))]
    return pl.pallas_call(
        _merge_kernel,
        grid=(R // tm, nj),
        in_specs=specs,
        out_specs=pl.BlockSpec((tm, D), lambda i, j: (i, 0)),
        out_shape=jax.ShapeDtypeStruct((R, D), F32),
        scratch_shapes=[pltpu.VMEM((tm, D), F32)],
        compiler_params=_params("parallel", "arbitrary"),
        name="merge",
    )(*branches, gates, gates, gates, gates, w_branch, w_branch, w_branch, w_branch, w_out)


def _moe_dense_kernel(h_ref, comb_ref, w1_ref, w3_ref, w2_ref, o_ref, acc_ref):
    e = pl.program_id(1)
    h = h_ref[...]
    a = jnp.dot(h, w1_ref[...], preferred_element_type=F32)
    b = jnp.dot(h, w3_ref[...], preferred_element_type=F32)
    hid = (a * jax.nn.sigmoid(a) * b).astype(BF16)
    y = jnp.dot(hid, w2_ref[...], preferred_element_type=F32)
    comb = comb_ref[...]
    lane = lax.broadcasted_iota(jnp.int32, comb.shape, 1)
    ce = jnp.sum(jnp.where(lane == e, comb, 0.0), axis=1, keepdims=True)

    @pl.when(e == 0)
    def _():
        acc_ref[...] = ce * y

    @pl.when(e > 0)
    def _():
        acc_ref[...] += ce * y

    @pl.when(e == N_EXPERTS - 1)
    def _():
        o_ref[...] = acc_ref[...]


def _moe_dense(h, comb, w1, w3, w2):
    R, D = h.shape
    tm = ROW_TILE
    return pl.pallas_call(
        _moe_dense_kernel,
        grid=(R // tm, N_EXPERTS),
        in_specs=[pl.BlockSpec((tm, D), lambda i, e: (i, 0)),
                  pl.BlockSpec((tm, N_EXPERTS), lambda i, e: (i, 0)),
                  pl.BlockSpec((None, D, D_FF), lambda i, e: (e, 0, 0)),
                  pl.BlockSpec((None, D, D_FF), lambda i, e: (e, 0, 0)),
                  pl.BlockSpec((None, D_FF, D), lambda i, e: (e, 0, 0))],
        out_specs=pl.BlockSpec((tm, D), lambda i, e: (i, 0)),
        out_shape=jax.ShapeDtypeStruct((R, D), F32),
        scratch_shapes=[pltpu.VMEM((tm, D), F32)],
        compiler_params=_params("parallel", "arbitrary"),
        name="moe_dense",
    )(h, comb, w1, w3, w2)


def _pad_heads(w, heads, width, lo, hi):
    K = w.shape[0]
    w = w.reshape(K, heads, width)[:, :, lo:hi]
    w = jnp.pad(w, ((0, 0), (0, 0), (0, HEAD_PAD - (hi - lo))))
    return w.reshape(K, heads * HEAD_PAD)


def kernel(x, c, ctx, c_ctx, w_ada, b_ada, w_in, mla_q_norm, mla_w_uq, mla_kv_norm, mla_w_ukv, diff_lq1, diff_lk1,
           diff_lq2, diff_lk2, diff_subln, conv_w, conv_b, conv_ln_g, conv_ln_b, w_branch, w_out, ln1_g, ln1_b,
           ln2_g, ln2_b, router_w, router_bias, exp_w1, exp_w3, exp_w2):
    B, S, D = x.shape
    C = ctx.shape[1]
    depth = w_in.shape[0]
    N, NC = B * S, B * C
    tm = ROW_TILE
    assert D == D_MODEL and S % tm == 0 and NC % tm == 0 and S % C == 0 and S % GRID_W == 0

    def seg(i):
        return jnp.minimum(i * tm // S, B)

    def pos_block(i):
        return jnp.where(i < N // tm, i % (S // tm), S // tm)

    rows = S // GRID_W
    tabs_mq = _rope_tables(rows, MLA_DR, MLA_DN, HEAD_PAD, MLA_HEADS, tm)
    tabs_mk = _rope_tables(rows, MLA_DR, 0, LANES, 1, tm)
    tabs_d = _rope_tables(rows, DIFF_D, 0, DIFF_D, 2 * DIFF_HEADS, tm)

    nseg = 8
    cond = jnp.zeros((nseg, D), F32).at[:B].set(c).at[B].set(c_ctx)
    mod_all = _ada(cond, w_ada, b_ada).reshape(depth, nseg * 6, 1, D)

    dft_ch = _dft_channel_matrix().astype(BF16)
    dft_lat = _dft_position_matrix(S).astype(BF16)
    dft_ctx = _dft_position_matrix(C).astype(BF16)
    rw_t = router_w.T
    rw_hi = rw_t.astype(BF16)
    rw_lo = (rw_t - rw_hi.astype(F32)).astype(BF16)
    router = (rw_hi, rw_lo, router_bias.reshape(N_EXPERTS, 1))

    xs = jnp.concatenate([x.reshape(N, D), ctx.reshape(NC, D)], axis=0)
    alpha = (2.0 * depth) ** 0.25
    xs, h = _norm(xs, seg=seg, alpha=alpha, first=True, mod_h=mod_all[0], k_scale=1, k_shift=0, name="norm_in")

    for l in range(depth):
        mod = mod_all[l]
        wl = w_in[l]
        w_mla = jnp.pad(wl[:, :OFF_DIFF], ((0, 0), (0, MLA_COLS_PAD - MLA_COLS))).astype(BF16)
        cm = _project(h, w_mla, F32, "proj_mla", MLA_COLS_PAD)
        cd = _project(h, wl[:, OFF_DIFF:OFF_CONV].astype(BF16), BF16, "proj_diff", DIFF_COLS)
        cc = _project(h, wl[:, OFF_CONV:OFF_FNET].astype(BF16), BF16, "proj_conv", CONV_COLS)
        cf = _project(h, wl[:, OFF_FNET:OFF_GATE].astype(BF16), BF16, "proj_fnet", FNET_COLS)
        gates = _project(h, wl[:, OFF_GATE:].astype(BF16), BF16, "proj_gate", 2048)

        wq = _pad_heads(mla_w_uq[l], MLA_HEADS, MLA_DN + MLA_DR, 0, MLA_DN + MLA_DR).astype(BF16)
        wk = _pad_heads(mla_w_ukv[l], MLA_HEADS, MLA_DN + MLA_DV, 0, MLA_DN).astype(BF16)
        wv = _pad_heads(mla_w_ukv[l], MLA_HEADS, MLA_DN + MLA_DV, MLA_DN, MLA_DN + MLA_DV).astype(BF16)
        mq, mk, mv = _mla_prep(cm, mla_q_norm[l].reshape(1, -1), mla_kv_norm[l].reshape(1, -1), wq, wk, wv,
                               tabs_mq, tabs_mk, pos_block)
        o_mla = jnp.concatenate([_attention("mla", mq, mk, mv, B=B, S=S, C=C, ctx_queries=False),
                                 _attention("mla", mq, mk, mv, B=B, S=S, C=C, ctx_queries=True)], axis=0)

        lam_init = 0.8 - 0.6 * math.exp(-0.3 * l)
        ll = jnp.stack([diff_lq1[l], diff_lk1[l], diff_lq2[l], diff_lk2[l]], axis=0)
        sub = diff_subln[l].reshape(1, -1)
        dq, dk, dv = _diff_prep(cd, tabs_d, pos_block)
        o_diff = jnp.concatenate(
            [_attention("diff", dq, dk, dv, B=B, S=S, C=C, ctx_queries=False, extra=(ll, sub), lam_init=lam_init),
             _attention("diff", dq, dk, dv, B=B, S=S, C=C, ctx_queries=True, extra=(ll, sub), lam_init=lam_init)],
            axis=0)

        o_conv = jnp.concatenate([_conv(cc, 0, B, S, conv_w[l], conv_b[l], conv_ln_g[l], conv_ln_b[l]),
                                  _conv(cc, N, B, C, conv_w[l], conv_b[l], conv_ln_g[l], conv_ln_b[l])], axis=0)
        y = _project(cf, dft_ch, BF16, "fnet_ch", 2 * FNET_COLS)
        o_fnet = jnp.concatenate([_fnet_positions(y, dft_lat, 0, B, S), _fnet_positions(y, dft_ctx, N, B, C)], axis=0)

        mix = _merge([o_mla, o_diff, o_conv, o_fnet], gates, w_branch[l].astype(BF16), w_out[l].astype(BF16))
        xs, h2, comb_t = _norm(xs, seg=seg, alpha=alpha, m=mix, mod_g=mod, k_gate=2, ln_g=ln1_g[l], ln_b=ln1_b[l],
                               mod_h=mod, k_scale=4, k_shift=3, router=router, name="norm1")
        f = _moe_dense(h2, comb_t.T, exp_w1[l].astype(BF16), exp_w3[l].astype(BF16), exp_w2[l].astype(BF16))
        if l + 1 < depth:
            xs, h = _norm(xs, seg=seg, alpha=alpha, m=f, mod_g=mod, k_gate=5, ln_g=ln2_g[l], ln_b=ln2_b[l],
                          mod_h=mod_all[l + 1], k_scale=1, k_shift=0, name="norm2")
        else:
            (xs,) = _norm(xs, seg=seg, alpha=alpha, m=f, mod_g=mod, k_gate=5, ln_g=ln2_g[l], ln_b=ln2_b[l],
                          name="norm_out")
    return xs[:N].reshape(B, S, D)
```

```python
import functools
import math

import jax
import jax.numpy as jnp
from jax import lax
from jax.experimental import pallas as pl
from jax.experimental.pallas import tpu as pltpu

F32 = jnp.float32
BF16 = jnp.bfloat16

D_MODEL = 2048
GRID_W = 64
ROPE_THETA = 10000.0
EPS = 1e-6

MLA_HEADS = 8
MLA_DN = 64
MLA_DR = 32
MLA_DV = 64
MLA_Q_RANK = 384
MLA_KV_RANK = 256
DIFF_HEADS = 4
DIFF_D = 64
CONV_W = 512
CONV_K = 31
FNET_GROUPS = 4
FNET_GROUP_W = 128
N_BRANCH = 4
BRANCH_W = 512
N_EXPERTS = 16
N_GROUPS = 4
EXPERTS_PER_GROUP = N_EXPERTS // N_GROUPS
D_FF = 1024

MLA_COLS = MLA_Q_RANK + MLA_KV_RANK + MLA_DR
DIFF_COLS = 3 * DIFF_HEADS * 2 * DIFF_D
CONV_COLS = 2 * CONV_W
FNET_COLS = FNET_GROUPS * FNET_GROUP_W
GATE_COLS = N_BRANCH * D_MODEL
OFF_DIFF = MLA_COLS
OFF_CONV = OFF_DIFF + DIFF_COLS
OFF_FNET = OFF_CONV + CONV_COLS
OFF_GATE = OFF_FNET + FNET_COLS

LANES = 128
HEAD_PAD = 128
MLA_COLS_PAD = 768
CONV_HALO = 16
VMEM_LIMIT = 56 * 1024 * 1024
LOG2E = math.log2(math.e)

ROW_TILE = 512
MOE_TILE = 512
COMBINE_TILE = 256
ATTN_TQ = 512
ATTN_TK = 1024
ATTN_UNROLL = 8


def _params(*sem):
    return pltpu.CompilerParams(dimension_semantics=("arbitrary",) * len(sem), vmem_limit_bytes=VMEM_LIMIT)


def _mm_kernel(a_ref, w_ref, o_ref, acc_ref, *, nk):
    prod = jnp.dot(a_ref[...], w_ref[...], preferred_element_type=F32)
    if nk == 1:
        o_ref[...] = prod.astype(o_ref.dtype)
        return
    k = pl.program_id(3)

    @pl.when(k == 0)
    def _():
        acc_ref[...] = prod

    @pl.when(k > 0)
    def _():
        acc_ref[...] += prod

    @pl.when(k == nk - 1)
    def _():
        o_ref[...] = acc_ref[...].astype(o_ref.dtype)


def _matmul(a, w, *, out_shape, out_dtype, tm, tn, tk, grid, a_map, w_map, o_map, name):
    nk = grid[3]
    return pl.pallas_call(
        functools.partial(_mm_kernel, nk=nk),
        grid=grid,
        in_specs=[pl.BlockSpec((tm, tk), a_map), pl.BlockSpec((tk, tn), w_map)],
        out_specs=pl.BlockSpec((tm, tn), o_map),
        out_shape=jax.ShapeDtypeStruct(out_shape, out_dtype),
        scratch_shapes=[pltpu.VMEM((tm, tn), F32)],
        compiler_params=_params("parallel", "parallel", "parallel", "arbitrary"),
        name=name,
    )(a, w)


def _project(h, w, out_dtype, name, tn):
    R, K = h.shape
    N = w.shape[1]
    tm = ROW_TILE
    return _matmul(h, w, out_shape=(R, N), out_dtype=out_dtype, tm=tm, tn=tn, tk=K,
                   grid=(R // tm, N // tn, 1, 1),
                   a_map=lambda i, j, u, k: (i, 0), w_map=lambda i, j, u, k: (0, j),
                   o_map=lambda i, j, u, k: (i, j), name=name)


def _ada_kernel(c_ref, w_ref, b_ref, o_ref):
    c = c_ref[...]
    a = (c * jax.nn.sigmoid(c)).astype(BF16)
    o_ref[...] = jnp.dot(a, w_ref[...].astype(BF16), preferred_element_type=F32) + b_ref[...]


def _ada(cond, w_ada, b_ada):
    L, D, N6 = w_ada.shape
    rows = cond.shape[0]
    tn = 1024
    return pl.pallas_call(
        _ada_kernel,
        grid=(L, N6 // tn),
        in_specs=[pl.BlockSpec((rows, D), lambda l, j: (0, 0)),
                  pl.BlockSpec((None, D, tn), lambda l, j: (l, 0, j)),
                  pl.BlockSpec((None, 1, tn), lambda l, j: (l, 0, j))],
        out_specs=pl.BlockSpec((None, rows, tn), lambda l, j: (l, 0, j)),
        out_shape=jax.ShapeDtypeStruct((L, rows, N6), F32),
        compiler_params=_params("parallel", "parallel"),
        name="ada_mod",
    )(cond, w_ada, b_ada.reshape(L, 1, N6))


def _route(logits_t, bias_t):
    scores = jax.nn.sigmoid(logits_t)
    sel = scores + bias_t
    sel_r = [sel[e:e + 1, :] for e in range(N_EXPERTS)]
    sc_r = [scores[e:e + 1, :] for e in range(N_EXPERTS)]
    per = EXPERTS_PER_GROUP
    gscore = []
    for g in range(N_GROUPS):
        r = sel_r[g * per:(g + 1) * per]
        best = None
        for a in range(per):
            for b in range(a + 1, per):
                s = r[a] + r[b]
                best = s if best is None else jnp.maximum(best, s)
        gscore.append(best)
    best_g = jnp.zeros_like(gscore[0], dtype=jnp.int32)
    best_v = gscore[0]
    for g in range(1, N_GROUPS):
        better = gscore[g] > best_v
        best_g = jnp.where(better, g, best_g)
        best_v = jnp.where(better, gscore[g], best_v)
    v = []
    s = []
    for j in range(per):
        vj = sel_r[j]
        sj = sc_r[j]
        for g in range(1, N_GROUPS):
            vj = jnp.where(best_g == g, sel_r[g * per + j], vj)
            sj = jnp.where(best_g == g, sc_r[g * per + j], sj)
        v.append(vj)
        s.append(sj)
    chosen = []
    for j in range(per):
        rank = jnp.zeros_like(best_g)
        for i in range(per):
            if i == j:
                continue
            ahead = (v[i] > v[j]) | ((v[i] == v[j]) & (i < j))
            rank = rank + ahead.astype(jnp.int32)
        chosen.append(rank < 2)
    total = jnp.zeros_like(s[0])
    for j in range(per):
        total = total + jnp.where(chosen[j], s[j], 0.0)
    picked, weight = [], []
    for e in range(N_EXPERTS):
        g, j = divmod(e, per)
        on = (best_g == g) & chosen[j]
        picked.append(on)
        weight.append(jnp.where(on, s[j] / total, 0.0))
    return picked, weight


def _dispatch_info(picked, weight, tri_ref, tot_ref):
    sel = jnp.concatenate([p.astype(F32) for p in picked], axis=0)
    cum = jnp.dot(sel.astype(BF16), tri_ref[...], preferred_element_type=F32)
    tot = tot_ref[...]
    rank = cum + tot[:, 0:1]
    tot_ref[...] = tot + jnp.sum(sel, axis=1, keepdims=True)
    zero = jnp.zeros_like(weight[0])
    seen = zero
    rec = [zero] * 6
    for e in range(N_EXPERTS):
        on = picked[e]
        for slot, is_slot in ((0, on & (seen == 0.0)), (1, on & (seen == 1.0))):
            rec[slot] = jnp.where(is_slot, float(e), rec[slot])
            rec[2 + slot] = jnp.where(is_slot, rank[e:e + 1, :], rec[2 + slot])
            rec[4 + slot] = jnp.where(is_slot, weight[e], rec[4 + slot])
        seen = seen + on.astype(F32)
    return jnp.concatenate(rec + [zero, zero], axis=0)


def _standardize(y):
    mu = jnp.mean(y, axis=-1, keepdims=True)
    d = y - mu
    var = jnp.mean(d * d, axis=-1, keepdims=True)
    return d * lax.rsqrt(var + EPS)


def _norm_kernel(*refs, first, emit_h, router, alpha):
    refs = list(refs)
    x_ref = refs.pop(0)
    if not first:
        m_ref, g_ref, lng_ref, lnb_ref = refs[:4]
        refs = refs[4:]
    if emit_h:
        sc_ref, sh_ref = refs[:2]
        refs = refs[2:]
    if router:
        rwh_ref, rwl_ref, rb_ref, tri_ref = refs[:4]
        refs = refs[4:]
    xo_ref = refs.pop(0)
    x = x_ref[...]
    xn = _standardize(x if first else alpha * x + g_ref[...] * m_ref[...])
    if not first:
        xn = xn * lng_ref[...] + lnb_ref[...]
    xo_ref[...] = xn
    if not emit_h:
        return
    h_ref = refs.pop(0)
    h = xn * (1.0 + sc_ref[...]) + sh_ref[...]
    h_ref[...] = h.astype(h_ref.dtype)
    if router:
        rec_ref, cnt_ref, tot_ref = refs

        @pl.when(pl.program_id(0) == 0)
        def _():
            tot_ref[...] = jnp.zeros_like(tot_ref)

        h_hi = h.astype(BF16)
        h_lo = (h - h_hi.astype(F32)).astype(BF16)
        nt = (((1,), (1,)), ((), ()))
        logits_t = (lax.dot_general(rwh_ref[...], h_hi, nt, preferred_element_type=F32)
                    + lax.dot_general(rwh_ref[...], h_lo, nt, preferred_element_type=F32)
                    + lax.dot_general(rwl_ref[...], h_hi, nt, preferred_element_type=F32))
        picked, weight = _route(logits_t, rb_ref[...])
        rec_ref[...] = _dispatch_info(picked, weight, tri_ref, tot_ref)
        cnt_ref[...] = tot_ref[...]


def _norm(x, *, seg, alpha, first=False, m=None, mod_g=None, k_gate=None, ln_g=None, ln_b=None,
          mod_h=None, k_scale=None, k_shift=None, router=None, name):
    R, D = x.shape
    tm = ROW_TILE
    emit_h = mod_h is not None
    row = pl.BlockSpec((tm, D), lambda i: (i, 0))

    def mod_spec(k):
        return pl.BlockSpec((None, 1, D), lambda i: (seg(i) * 6 + k, 0, 0))

    vec = pl.BlockSpec((1, D), lambda i: (0, 0))
    args, specs = [x], [row]
    if not first:
        args += [m, mod_g, ln_g.reshape(1, D), ln_b.reshape(1, D)]
        specs += [row, mod_spec(k_gate), vec, vec]
    if emit_h:
        args += [mod_h, mod_h]
        specs += [mod_spec(k_scale), mod_spec(k_shift)]
    scratch = []
    if router is not None:
        rwh, rwl, rb = router
        tri = (jnp.arange(tm)[:, None] <= jnp.arange(tm)[None, :]).astype(BF16)
        args += [rwh, rwl, rb, tri]
        specs += [pl.BlockSpec((N_EXPERTS, D), lambda i: (0, 0))] * 2
        specs += [pl.BlockSpec((N_EXPERTS, 1), lambda i: (0, 0)), pl.BlockSpec((tm, tm), lambda i: (0, 0))]
        scratch = [pltpu.VMEM((N_EXPERTS, LANES), F32)]
    out_shape = [jax.ShapeDtypeStruct((R, D), F32)]
    out_specs = [row]
    if emit_h:
        out_shape.append(jax.ShapeDtypeStruct((R, D), BF16 if router is None else F32))
        out_specs.append(row)
    if router is not None:
        out_shape += [jax.ShapeDtypeStruct((8, R), F32), jax.ShapeDtypeStruct((N_EXPERTS, LANES), F32)]
        out_specs += [pl.BlockSpec((8, tm), lambda i: (0, i)), pl.BlockSpec((N_EXPERTS, LANES), lambda i: (0, 0))]
    return pl.pallas_call(
        functools.partial(_norm_kernel, first=first, emit_h=emit_h, router=router is not None, alpha=alpha),
        grid=(R // tm,),
        in_specs=specs,
        out_specs=out_specs,
        out_shape=out_shape,
        scratch_shapes=scratch,
        compiler_params=_params("arbitrary" if router is not None else "parallel"),
        name=name,
    )(*args)


def _rope_tables(rows, rot_dim, lane0, chunk, reps, ident_rows):
    n = rows * GRID_W
    r = jnp.broadcast_to(jnp.arange(rows, dtype=F32)[:, None], (rows, GRID_W)).reshape(n)
    c = jnp.broadcast_to(jnp.arange(GRID_W, dtype=F32)[None, :], (rows, GRID_W)).reshape(n)
    quarter = rot_dim // 4
    inv_freq = ROPE_THETA ** (-jnp.arange(quarter, dtype=F32) / quarter)
    ar = r[:, None] * inv_freq
    ac = c[:, None] * inv_freq
    ang = jnp.concatenate([ar, ar, ac, ac], -1)
    sign = jnp.concatenate([-jnp.ones((quarter,), F32), jnp.ones((quarter,), F32)] * 2)
    cos = jnp.ones((n, chunk), F32).at[:, lane0:lane0 + rot_dim].set(jnp.cos(ang))
    sin = jnp.zeros((n, chunk), F32).at[:, lane0:lane0 + rot_dim].set(jnp.sin(ang) * sign)
    cos = jnp.concatenate([cos, jnp.ones((ident_rows, chunk), F32)], 0)
    sin = jnp.concatenate([sin, jnp.zeros((ident_rows, chunk), F32)], 0)
    return jnp.tile(cos, (1, reps)), jnp.tile(sin, (1, reps))


def _rope(x, cos, sin_signed, quarter, lane0):
    w = x.shape[-1]
    lane = lax.broadcasted_iota(jnp.int32, x.shape, 1)
    even = (((lane - lane0) // quarter) % 2) == 0
    partner = jnp.where(even, pltpu.roll(x, w - quarter, 1), pltpu.roll(x, quarter, 1))
    return x * cos + partner * sin_signed


def _rms(x, g):
    return x * lax.rsqrt(jnp.mean(x * x, axis=-1, keepdims=True) + EPS) * g


def _mla_prep_kernel(cm_ref, qn_ref, kvn_ref, wq_ref, wk_ref, wv_ref, cq_ref, sq_ref, ck_ref, sk_ref,
                     q_ref, k_ref, v_ref):
    cm = cm_ref[...]
    cq = cm[:, :MLA_Q_RANK]
    ckv = cm[:, MLA_Q_RANK:MLA_Q_RANK + MLA_KV_RANK]
    kr = cm[:, MLA_Q_RANK + MLA_KV_RANK:]
    qn = _rms(cq, qn_ref[...]).astype(BF16)
    kvn = _rms(ckv, kvn_ref[...]).astype(BF16)
    q = jnp.dot(qn, wq_ref[...], preferred_element_type=F32)
    q = _rope(q, cq_ref[...], sq_ref[...], MLA_DR // 4, MLA_DN)
    scale = (MLA_DN + MLA_DR) ** -0.5 * LOG2E
    q_ref[...] = (q * scale).astype(BF16)
    kr = _rope(kr, ck_ref[...], sk_ref[...], MLA_DR // 4, 0)
    kr = pltpu.roll(kr, MLA_DN, 1)
    k = jnp.dot(kvn, wk_ref[...], preferred_element_type=F32)
    k_ref[...] = (k + jnp.concatenate([kr] * MLA_HEADS, axis=1)).astype(BF16)
    v = jnp.dot(kvn, wv_ref[...], preferred_element_type=F32)
    lane = lax.broadcasted_iota(jnp.int32, v.shape, 1)
    v_ref[...] = jnp.where(lane % HEAD_PAD == MLA_DV, 1.0, v).astype(BF16)


def _mla_prep(cm, q_norm, kv_norm, wq, wk, wv, tabs_q, tabs_k, pos_block):
    R = cm.shape[0]
    tm = ROW_TILE
    W = MLA_HEADS * HEAD_PAD
    row = lambda w: pl.BlockSpec((tm, w), lambda i: (i, 0))
    full = lambda a: pl.BlockSpec(a.shape, lambda i: (0, 0))
    tab = lambda w: pl.BlockSpec((tm, w), lambda i: (pos_block(i), 0))
    outs = pl.pallas_call(
        _mla_prep_kernel,
        grid=(R // tm,),
        in_specs=[row(MLA_COLS_PAD), full(q_norm), full(kv_norm), full(wq), full(wk), full(wv),
                  tab(W), tab(W), tab(LANES), tab(LANES)],
        out_specs=[row(W)] * 3,
        out_shape=[jax.ShapeDtypeStruct((R, W), BF16)] * 3,
        compiler_params=_params("parallel"),
        name="mla_prep",
    )(cm, q_norm, kv_norm, wq, wk, wv, tabs_q[0], tabs_q[1], tabs_k[0], tabs_k[1])
    return outs


def _diff_prep_kernel(cd_ref, cos_ref, sin_ref, q_ref, k_ref, v_ref):
    w = DIFF_HEADS * 2 * DIFF_D
    cd = cd_ref[...].astype(F32)
    cos = cos_ref[...]
    sin = sin_ref[...]
    q = _rope(cd[:, :w], cos, sin, DIFF_D // 4, 0)
    q_ref[...] = (q * (DIFF_D ** -0.5 * LOG2E)).astype(BF16)
    k_ref[...] = _rope(cd[:, w:2 * w], cos, sin, DIFF_D // 4, 0).astype(BF16)
    v = cd_ref[:, 2 * w:]
    lane = lax.broadcasted_iota(jnp.int32, (v.shape[0], LANES), 1)
    ones = jnp.where(lane == 0, 1.0, 0.0).astype(BF16)
    parts = []
    for h in range(DIFF_HEADS):
        parts += [v[:, h * 2 * DIFF_D:(h + 1) * 2 * DIFF_D], ones]
    v_ref[...] = jnp.concatenate(parts, axis=1)


def _diff_prep(cd, tabs, pos_block):
    R = cd.shape[0]
    tm = ROW_TILE
    w = DIFF_HEADS * 2 * DIFF_D
    row = lambda c: pl.BlockSpec((tm, c), lambda i: (i, 0))
    tab = pl.BlockSpec((tm, w), lambda i: (pos_block(i), 0))
    return pl.pallas_call(
        _diff_prep_kernel,
        grid=(R // tm,),
        in_specs=[row(DIFF_COLS), tab, tab],
        out_specs=[row(w), row(w), row(2 * w)],
        out_shape=[jax.ShapeDtypeStruct((R, w), BF16), jax.ShapeDtypeStruct((R, w), BF16),
                   jax.ShapeDtypeStruct((R, 2 * w), BF16)],
        compiler_params=_params("parallel"),
        name="diff_prep",
    )(cd, tabs[0], tabs[1])


def _flash(streams, sources, vw, unroll):
    tq = streams[0][0].shape[0]
    carry = tuple((jnp.full((tq, 1), -jnp.inf, F32), jnp.zeros((tq, vw), F32)) for _ in streams)
    for k_ref, v_ref, length, tk in sources:
        def step(c, carry, k_ref=k_ref, v_ref=v_ref, tk=tk):
            r0 = pl.multiple_of(c * tk, tk)
            out = []
            for (q, kl0, vl0), (m, acc) in zip(streams, carry):
                k = k_ref[pl.ds(r0, tk), kl0:kl0 + HEAD_PAD]
                v = v_ref[pl.ds(r0, tk), vl0:vl0 + vw]
                s = lax.dot_general(q, k, (((1,), (1,)), ((), ())), preferred_element_type=F32)
                m_new = jnp.maximum(m, jnp.max(s, axis=1, keepdims=True))
                p = jnp.exp2(s - m_new)
                a = jnp.exp2(m - m_new)
                acc = a * acc + jnp.dot(p.astype(BF16), v, preferred_element_type=F32)
                out.append((m_new, acc))
            return tuple(out)
        n = length // tk
        if n == 1:
            carry = step(0, carry)
        else:
            carry = lax.fori_loop(0, n, step, carry, unroll=unroll)
    return [acc for _, acc in carry]


def _mla_attn_kernel(*refs, n_src, lens, tk, unroll):
    q_ref = refs[0]
    kv = refs[1:1 + 2 * n_src]
    o_ref = refs[1 + 2 * n_src]
    streams = [(q_ref[:, hh * HEAD_PAD:(hh + 1) * HEAD_PAD], hh * HEAD_PAD, hh * HEAD_PAD) for hh in range(2)]
    sources = [(kv[2 * s], kv[2 * s + 1], lens[s], min(tk, lens[s])) for s in range(n_src)]
    accs = _flash(streams, sources, HEAD_PAD, unroll)
    outs = [acc[:, :MLA_DV] / acc[:, MLA_DV:MLA_DV + 1] for acc in accs]
    o_ref[...] = jnp.concatenate(outs, axis=1).astype(BF16)


def _diff_attn_kernel(*refs, n_src, lens, tk, unroll, lam_init):
    q_ref, ll_ref, sub_ref = refs[:3]
    kv = refs[3:3 + 2 * n_src]
    o_ref = refs[3 + 2 * n_src]
    ll = ll_ref[...]
    lam = (jnp.exp(jnp.sum(ll[0:1, :] * ll[1:2, :], axis=1, keepdims=True))
           - jnp.exp(jnp.sum(ll[2:3, :] * ll[3:4, :], axis=1, keepdims=True)) + lam_init)
    q = q_ref[...]
    lane = lax.broadcasted_iota(jnp.int32, q.shape, 1)
    vw = 2 * HEAD_PAD
    dv = 2 * DIFF_D
    streams = [(jnp.where((lane // DIFF_D) == mp, q, jnp.zeros_like(q)), 0, 0) for mp in range(2)]
    sources = [(kv[2 * s], kv[2 * s + 1], lens[s], min(tk, lens[s])) for s in range(n_src)]
    accs = _flash(streams, sources, vw, unroll)
    o = [acc[:, :dv] / acc[:, dv:dv + 1] for acc in accs]
    od = o[0] - lam * o[1]
    o_ref[...] = (_rms(od, sub_ref[...]) * (1.0 - lam_init)).astype(BF16)


def _attention(kind, q, k, v, *, B, S, C, ctx_queries, extra=None, lam_init=None):
    N = B * S
    tk = ATTN_TK
    unroll = ATTN_UNROLL
    if kind == "mla":
        steps, qw, kw, vw, ow = MLA_HEADS // 2, 2 * HEAD_PAD, 2 * HEAD_PAD, 2 * HEAD_PAD, 2 * MLA_DV
    else:
        steps, qw, kw, vw, ow = DIFF_HEADS, HEAD_PAD, HEAD_PAD, 2 * HEAD_PAD, 2 * DIFF_D
    cblk = N // C
    if ctx_queries:
        tq, nq, rows_out = C, 1, B * C
        q_spec = pl.BlockSpec((tq, qw), lambda b, h, i: (cblk + b, h))
        kv_specs = [pl.BlockSpec((C, kw), lambda b, h, i: (cblk + b, h)),
                    pl.BlockSpec((C, vw), lambda b, h, i: (cblk + b, h))]
        lens = (C,)
    else:
        tq = ATTN_TQ
        nq, rows_out = S // tq, N
        q_spec = pl.BlockSpec((tq, qw), lambda b, h, i: (b * nq + i, h))
        kv_specs = [pl.BlockSpec((S, kw), lambda b, h, i: (b, h)),
                    pl.BlockSpec((S, vw), lambda b, h, i: (b, h)),
                    pl.BlockSpec((C, kw), lambda b, h, i: (cblk + b, h)),
                    pl.BlockSpec((C, vw), lambda b, h, i: (cblk + b, h))]
        lens = (S, C)
    n_src = len(lens)
    kv_args = [k, v] * n_src
    o_spec = pl.BlockSpec((tq, ow), lambda b, h, i: (b * nq + i, h))
    if kind == "mla":
        body = functools.partial(_mla_attn_kernel, n_src=n_src, lens=lens, tk=tk, unroll=unroll)
        args, specs = [q], [q_spec]
    else:
        ll, sub = extra
        body = functools.partial(_diff_attn_kernel, n_src=n_src, lens=lens, tk=tk, unroll=unroll, lam_init=lam_init)
        args = [q, ll, sub]
        specs = [q_spec, pl.BlockSpec(ll.shape, lambda b, h, i: (0, 0)), pl.BlockSpec(sub.shape, lambda b, h, i: (0, 0))]
    return pl.pallas_call(
        body,
        grid=(B, steps, nq),
        in_specs=specs + kv_specs,
        out_specs=o_spec,
        out_shape=jax.ShapeDtypeStruct((rows_out, steps * ow), BF16),
        compiler_params=_params("parallel", "parallel", "parallel"),
        name=f"{kind}_attn_{'ctx' if ctx_queries else 'lat'}",
    )(*args, *kv_args)


def _conv_kernel(main_ref, prev_ref, next_ref, w_ref, b_ref, g_ref, bb_ref, o_ref, z_ref, *, ts, nt):
    i = pl.program_id(1)

    def glu(blk):
        a = blk[:, :CONV_W].astype(F32)
        g = blk[:, CONV_W:].astype(F32)
        return a * jax.nn.sigmoid(g)

    H = CONV_HALO
    z_ref[0:H, :] = glu(prev_ref[...]) * (i > 0).astype(F32)
    z_ref[H:H + ts, :] = glu(main_ref[...])
    z_ref[H + ts:2 * H + ts, :] = glu(next_ref[...]) * (i < nt - 1).astype(F32)
    rc = 32
    first = H - CONV_K // 2
    for r in range(ts // rc):
        acc = jnp.broadcast_to(b_ref[...], (rc, CONV_W))
        for k in range(CONV_K):
            acc = acc + w_ref[k:k + 1, :] * z_ref[r * rc + k + first:r * rc + k + first + rc, :]
        mu = jnp.mean(acc, axis=-1, keepdims=True)
        d = acc - mu
        var = jnp.mean(d * d, axis=-1, keepdims=True)
        y = d * lax.rsqrt(var + EPS) * g_ref[...] + bb_ref[...]
        o_ref[r * rc:(r + 1) * rc, :] = (y * jax.nn.sigmoid(y)).astype(BF16)


def _conv(cols, row_off, nb, L, conv_w, conv_b, ln_g, ln_b):
    R = cols.shape[0]
    ts = min(ROW_TILE, L)
    nt = L // ts
    H = CONV_HALO
    hb = ts // H
    base = lambda b, i: (row_off + b * L + i * ts) // H
    vec = lambda a: pl.BlockSpec(a.shape, lambda b, i: (0, 0))
    cb, g, bb = conv_b.reshape(1, CONV_W), ln_g.reshape(1, CONV_W), ln_b.reshape(1, CONV_W)
    return pl.pallas_call(
        functools.partial(_conv_kernel, ts=ts, nt=nt),
        grid=(nb, nt),
        in_specs=[pl.BlockSpec((ts, CONV_COLS), lambda b, i: ((row_off + b * L) // ts + i, 0)),
                  pl.BlockSpec((H, CONV_COLS), lambda b, i: (jnp.maximum(base(b, i) - 1, 0), 0)),
                  pl.BlockSpec((H, CONV_COLS), lambda b, i: (jnp.minimum(base(b, i) + hb, R // H - 1), 0)),
                  vec(conv_w), vec(cb), vec(g), vec(bb)],
        out_specs=pl.BlockSpec((ts, CONV_W), lambda b, i: (b * nt + i, 0)),
        out_shape=jax.ShapeDtypeStruct((nb * L, CONV_W), BF16),
        scratch_shapes=[pltpu.VMEM((ts + 2 * H, CONV_W), F32)],
        compiler_params=_params("parallel", "parallel"),
        name=f"conv_{L}",
    )(cols, cols, cols, conv_w, cb, g, bb)


def _dft_channel_matrix():
    j = jnp.arange(FNET_GROUP_W, dtype=jnp.int32)
    ang = (2.0 * math.pi / FNET_GROUP_W) * ((j[:, None] * j[None, :]) % FNET_GROUP_W).astype(F32)
    eye = jnp.eye(FNET_GROUPS, dtype=F32)
    scale = FNET_GROUP_W ** -0.5
    return jnp.concatenate([jnp.kron(eye, jnp.cos(ang)), jnp.kron(eye, jnp.sin(ang))], axis=1) * scale


def _dft_position_matrix(L):
    n = jnp.arange(L, dtype=jnp.int32)
    ang = (2.0 * math.pi / L) * ((n[:, None] * n[None, :]) % L).astype(F32)
    return jnp.concatenate([jnp.cos(ang), -jnp.sin(ang)], axis=1) * (L ** -0.5)


def _fnet_positions(y, a, row_off, nb, L):
    tm = min(ROW_TILE, L)
    tk = min(2048, L)
    kb = L // tk
    off = row_off // tk
    return _matmul(a, y, out_shape=(nb * L, FNET_COLS), out_dtype=F32, tm=tm, tn=FNET_COLS, tk=tk,
                   grid=(nb, L // tm, 1, 2 * kb),
                   a_map=lambda b, i, u, k: (i, k),
                   w_map=lambda b, i, u, k: (off + b * kb + k % kb, k // kb),
                   o_map=lambda b, i, u, k: (b * (L // tm) + i, 0),
                   name=f"fnet_pos_{L}")


FFT_MINOR = 128
FFT_GROUP = 8
FFT_COLS = 8


def _fft_major_matrix(n1):
    k = jnp.arange(n1, dtype=jnp.int32)
    ang = (2.0 * math.pi / n1) * ((k[:, None] * k[None, :]) % n1).astype(F32)
    return jnp.concatenate([jnp.cos(ang), jnp.sin(ang)], axis=0).astype(BF16)


def _fft_minor_matrices(L):
    n2c, grp = FFT_MINOR, FFT_GROUP
    n1 = L // n2c
    g = jnp.arange(n1 // grp, dtype=jnp.int32)[:, None, None, None, None]
    k2 = jnp.arange(n2c, dtype=jnp.int32)[None, :, None, None, None]
    j = jnp.arange(grp, dtype=jnp.int32)[None, None, :, None, None]
    jp = jnp.arange(grp, dtype=jnp.int32)[None, None, None, :, None]
    n2 = jnp.arange(n2c, dtype=jnp.int32)[None, None, None, None, :]
    phase = (n2 * k2 * n1 + n2 * (grp * g + j)) % L
    ang = (2.0 * math.pi / L) * phase.astype(F32)
    keep = (j == jp).astype(F32) * (L ** -0.5)
    shape = (n1 // grp, n2c * grp, grp * n2c)
    return ((jnp.cos(ang) * keep).reshape(shape).astype(BF16), (jnp.sin(ang) * keep).reshape(shape).astype(BF16))


def _fft_major_kernel(cs_ref, y_ref, a_ref, *, n1):
    pq = jnp.dot(cs_ref[...], y_ref[...], preferred_element_type=F32)
    p, q = pq[:n1], pq[n1:]
    w = FNET_COLS
    parts = []
    for t in range(FFT_COLS):
        re, im = slice(2 * t * w, (2 * t + 1) * w), slice((2 * t + 1) * w, (2 * t + 2) * w)
        parts += [p[:, re] - q[:, im], -(p[:, im] + q[:, re])]
    a_ref[...] = jnp.concatenate(parts, axis=1).astype(BF16)


def _fft_minor_kernel(mc_ref, ms_ref, a_ref, o_ref):
    w = FNET_COLS
    out = (jnp.dot(mc_ref[...], a_ref[:, :w], preferred_element_type=F32)
           + jnp.dot(ms_ref[...], a_ref[:, w:], preferred_element_type=F32))
    o_ref[...] = out.reshape(o_ref.shape)


def _fnet_positions_fft(y, nb, L):
    n2c, grp = FFT_MINOR, FFT_GROUP
    n1 = L // n2c
    w2 = 2 * FNET_COLS
    cs = _fft_major_matrix(n1)
    mc, ms = _fft_minor_matrices(L)
    flat = y[:nb * L].reshape(nb * n1, n2c * w2)
    tc = FFT_COLS * w2
    a = pl.pallas_call(
        functools.partial(_fft_major_kernel, n1=n1),
        grid=(nb, n2c // FFT_COLS),
        in_specs=[pl.BlockSpec((2 * n1, n1), lambda b, t: (0, 0)), pl.BlockSpec((n1, tc), lambda b, t: (b, t))],
        out_specs=pl.BlockSpec((n1, tc), lambda b, t: (b, t)),
        out_shape=jax.ShapeDtypeStruct((nb * n1, n2c * w2), BF16),
        compiler_params=_params("arbitrary", "arbitrary"),
        name="fft_major",
    )(cs, flat)
    a = a.reshape(nb * L, w2)
    rows = grp * n2c
    ng = n1 // grp
    out = pl.pallas_call(
        _fft_minor_kernel,
        grid=(ng, nb),
        in_specs=[pl.BlockSpec((None, rows, rows), lambda g, b: (g, 0, 0)),
                  pl.BlockSpec((None, rows, rows), lambda g, b: (g, 0, 0)),
                  pl.BlockSpec((rows, w2), lambda g, b: (b * ng + g, 0))],
        out_specs=pl.BlockSpec((None, n2c, grp, FNET_COLS), lambda g, b: (b, 0, g, 0)),
        out_shape=jax.ShapeDtypeStruct((nb, n2c, n1, FNET_COLS), F32),
        compiler_params=_params("arbitrary", "arbitrary"),
        name="fft_minor",
    )(mc, ms, a)
    return out.reshape(nb * L, FNET_COLS)


def _merge_kernel(*refs):
    br = refs[0:4]
    gt = refs[4:8]
    wb = refs[8:12]
    wo_ref, o_ref, acc_ref = refs[12:15]
    j = pl.program_id(1)
    mj = None
    for n in range(N_BRANCH):
        proj = jnp.dot(br[n][...].astype(BF16), wb[n][...], preferred_element_type=F32)
        t = jax.nn.sigmoid(gt[n][...].astype(F32)) * proj
        mj = t if mj is None else mj + t
    part = jnp.dot(mj.astype(BF16), wo_ref[...], preferred_element_type=F32)

    @pl.when(j == 0)
    def _():
        acc_ref[...] = part

    @pl.when(j > 0)
    def _():
        acc_ref[...] += part

    @pl.when(j == pl.num_programs(1) - 1)
    def _():
        o_ref[...] = acc_ref[...]


def _merge(branches, gates, w_branch, w_out):
    R = gates.shape[0]
    D = D_MODEL
    tm = ROW_TILE
    tj = BRANCH_W
    nj = D // tj
    specs = [pl.BlockSpec((tm, BRANCH_W), lambda i, j: (i, 0))] * 4
    specs += [pl.BlockSpec((tm, tj), functools.partial(lambda i, j, n: (i, n * nj + j), n=n)) for n in range(4)]
    specs += [pl.BlockSpec((None, BRANCH_W, tj), functools.partial(lambda i, j, n: (n, 0, j), n=n)) for n in range(4)]
    specs += [pl.BlockSpec((tj, D), lambda i, j: (j, 0))]
    return pl.pallas_call(
        _merge_kernel,
        grid=(R // tm, nj),
        in_specs=specs,
        out_specs=pl.BlockSpec((tm, D), lambda i, j: (i, 0)),
        out_shape=jax.ShapeDtypeStruct((R, D), F32),
        scratch_shapes=[pltpu.VMEM((tm, D), F32)],
        compiler_params=_params("parallel", "arbitrary"),
        name="merge",
    )(*branches, gates, gates, gates, gates, w_branch, w_branch, w_branch, w_branch, w_out)


def _moe_plan(rec, counts, R):
    T = MOE_TILE
    cnt = counts[:, 0].astype(jnp.int32)
    padded = ((cnt + T - 1) // T) * T
    ends = jnp.cumsum(padded)
    base = ends - padded
    n_tiles = (2 * R) // T + N_EXPERTS
    n_used = (ends[-1] // T).astype(jnp.int32)
    tile = jnp.minimum(jnp.arange(n_tiles, dtype=jnp.int32), n_used - 1)
    tile_expert = jnp.minimum(jnp.searchsorted(ends // T, tile, side="right"), N_EXPERTS - 1).astype(jnp.int32)
    e_ab = rec[0:2].astype(jnp.int32)
    r_ab = rec[2:4].astype(jnp.int32)
    pos = (base[e_ab] + r_ab - 1).T.reshape(2 * R)
    wgt = rec[4:6].T
    tail = jnp.concatenate([ends - T, (cnt > 0).astype(jnp.int32), n_used.reshape(1)]).astype(jnp.int32)
    return pos, wgt, tile_expert, n_used.reshape(1), tail


def _row_copy(src, src_row, dst, dst_row, sem):
    return pltpu.make_async_copy(src.at[pl.ds(src_row, 1)], dst.at[pl.ds(dst_row, 1)], sem)


def _dispatch_kernel(pos_ref, tail_ref, h_ref, buf_ref, zero_ref, sem, zsem, *, tm, n_tiles):
    i = pl.program_id(0)

    @pl.when(i == 0)
    def _():
        zero_ref[...] = jnp.zeros_like(zero_ref)

        def tail_copy(e):
            row = pl.multiple_of(tail_ref[e], MOE_TILE)
            return pltpu.make_async_copy(zero_ref, buf_ref.at[pl.ds(row, MOE_TILE)], zsem)

        for e in range(N_EXPERTS):
            @pl.when(tail_ref[N_EXPERTS + e] > 0)
            def _():
                tail_copy(e).start()
        for e in range(N_EXPERTS):
            @pl.when(tail_ref[N_EXPERTS + e] > 0)
            def _():
                tail_copy(e).wait()

        def unused_copy(t):
            return pltpu.make_async_copy(zero_ref, buf_ref.at[pl.ds(pl.multiple_of(t * MOE_TILE, MOE_TILE), MOE_TILE)],
                                         zsem)

        def clear(t, carry):
            unused_copy(t).start()
            unused_copy(t).wait()
            return carry

        lax.fori_loop(tail_ref[2 * N_EXPERTS], n_tiles, clear, 0)

    def issue(t, carry):
        for slot in range(2):
            _row_copy(h_ref, t, buf_ref, pos_ref[(i * tm + t) * 2 + slot], sem).start()
        return carry

    lax.fori_loop(0, tm, issue, 0, unroll=8)
    for slot in range(2):
        pltpu.make_async_copy(h_ref, buf_ref.at[pl.ds(0, tm)], sem).wait()


def _moe_dispatch(h, pos, tail, n_rows):
    R, D = h.shape
    tm = ROW_TILE
    return pl.pallas_call(
        functools.partial(_dispatch_kernel, tm=tm, n_tiles=n_rows // MOE_TILE),
        grid_spec=pltpu.PrefetchScalarGridSpec(
            num_scalar_prefetch=2,
            grid=(R // tm,),
            in_specs=[pl.BlockSpec((tm, D), lambda i, pos, tail: (i, 0))],
            out_specs=pl.BlockSpec(memory_space=pl.ANY),
            scratch_shapes=[pltpu.VMEM((MOE_TILE, D), F32), pltpu.SemaphoreType.DMA, pltpu.SemaphoreType.DMA],
        ),
        out_shape=jax.ShapeDtypeStruct((n_rows, D), F32),
        compiler_params=_params("arbitrary"),
        name="moe_dispatch",
    )(pos, tail, h)


def _experts_kernel(te_ref, nu_ref, x_ref, w1_ref, w3_ref, w2_ref, o_ref):
    @pl.when(pl.program_id(0) >= nu_ref[0])
    def _():
        o_ref[...] = jnp.zeros_like(o_ref)

    @pl.when(pl.program_id(0) < nu_ref[0])
    def _():
        x = x_ref[...].astype(BF16)
        a = jnp.dot(x, w1_ref[...], preferred_element_type=F32)
        b = jnp.dot(x, w3_ref[...], preferred_element_type=F32)
        hid = (a * jax.nn.sigmoid(a) * b).astype(BF16)
        o_ref[...] = jnp.dot(hid, w2_ref[...], preferred_element_type=F32)


def _moe_experts(xs, tile_expert, n_used, w1, w3, w2):
    P, D = xs.shape
    T = MOE_TILE
    row = pl.BlockSpec((T, D), lambda i, te, nu: (i, 0))
    return pl.pallas_call(
        _experts_kernel,
        grid_spec=pltpu.PrefetchScalarGridSpec(
            num_scalar_prefetch=2,
            grid=(P // T,),
            in_specs=[row,
                      pl.BlockSpec((None, D, D_FF), lambda i, te, nu: (te[i], 0, 0)),
                      pl.BlockSpec((None, D, D_FF), lambda i, te, nu: (te[i], 0, 0)),
                      pl.BlockSpec((None, D_FF, D), lambda i, te, nu: (te[i], 0, 0))],
            out_specs=row,
        ),
        out_shape=jax.ShapeDtypeStruct((P, D), F32),
        compiler_params=_params("arbitrary"),
        name="moe_experts",
    )(tile_expert, n_used, xs, w1, w3, w2)


def _combine_norm_kernel(*refs, tc, n_tiles, alpha, emit_h):
    refs = list(refs)
    pos_ref, x_ref, w_ref, g_ref, lng_ref, lnb_ref = refs[:6]
    refs = refs[6:]
    if emit_h:
        sc_ref, sh_ref = refs[:2]
        refs = refs[2:]
    y_ref, xo_ref = refs[:2]
    refs = refs[2:]
    if emit_h:
        h_ref = refs.pop(0)
    ya_ref, yb_ref, sem = refs
    i = pl.program_id(0)

    def gather(tile, slot):
        def issue(t, carry):
            p = (tile * tc + t) * 2
            _row_copy(y_ref, pos_ref[p], ya_ref.at[slot], t, sem.at[slot]).start()
            _row_copy(y_ref, pos_ref[p + 1], yb_ref.at[slot], t, sem.at[slot]).start()
            return carry
        lax.fori_loop(0, tc, issue, 0, unroll=8)

    @pl.when(i == 0)
    def _():
        gather(0, 0)

    @pl.when(i + 1 < n_tiles)
    def _():
        gather(i + 1, (i + 1) % 2)

    slot = i % 2
    for buf in (ya_ref, yb_ref):
        pltpu.make_async_copy(y_ref.at[pl.ds(0, tc)], buf.at[slot], sem.at[slot]).wait()
    w = w_ref[...]
    f = w[:, 0:1] * ya_ref[slot] + w[:, 1:2] * yb_ref[slot]
    xn = _standardize(alpha * x_ref[...] + g_ref[...] * f) * lng_ref[...] + lnb_ref[...]
    xo_ref[...] = xn
    if emit_h:
        h_ref[...] = (xn * (1.0 + sc_ref[...]) + sh_ref[...]).astype(BF16)


def _moe_combine_norm(x, ys, pos, wgt, *, seg_of_row, alpha, mod_g, k_gate, ln_g, ln_b, mod_h=None, k_scale=None,
                      k_shift=None, name):
    R, D = x.shape
    tc = COMBINE_TILE
    emit_h = mod_h is not None
    row = pl.BlockSpec((tc, D), lambda i, pos: (i, 0))

    def mod_spec(k):
        return pl.BlockSpec((None, 1, D), lambda i, pos: (seg_of_row(i * tc) * 6 + k, 0, 0))

    vec = pl.BlockSpec((1, D), lambda i, pos: (0, 0))
    args = [x, wgt, mod_g, ln_g.reshape(1, D), ln_b.reshape(1, D)]
    specs = [row, pl.BlockSpec((tc, 2), lambda i, pos: (i, 0)), mod_spec(k_gate), vec, vec]
    if emit_h:
        args += [mod_h, mod_h]
        specs += [mod_spec(k_scale), mod_spec(k_shift)]
    args.append(ys)
    specs.append(pl.BlockSpec(memory_space=pl.ANY))
    out_shape = [jax.ShapeDtypeStruct((R, D), F32)]
    out_specs = [row]
    if emit_h:
        out_shape.append(jax.ShapeDtypeStruct((R, D), BF16))
        out_specs.append(row)
    return pl.pallas_call(
        functools.partial(_combine_norm_kernel, tc=tc, n_tiles=R // tc, alpha=alpha, emit_h=emit_h),
        grid_spec=pltpu.PrefetchScalarGridSpec(
            num_scalar_prefetch=1,
            grid=(R // tc,),
            in_specs=specs,
            out_specs=out_specs,
            scratch_shapes=[pltpu.VMEM((2, tc, D), F32), pltpu.VMEM((2, tc, D), F32), pltpu.SemaphoreType.DMA((2,))],
        ),
        out_shape=out_shape,
        compiler_params=_params("arbitrary"),
        name=name,
    )(pos, *args)


def _pad_heads(w, heads, width, lo, hi):
    K = w.shape[0]
    w = w.reshape(K, heads, width)[:, :, lo:hi]
    w = jnp.pad(w, ((0, 0), (0, 0), (0, HEAD_PAD - (hi - lo))))
    return w.reshape(K, heads * HEAD_PAD)


def kernel(x, c, ctx, c_ctx, w_ada, b_ada, w_in, mla_q_norm, mla_w_uq, mla_kv_norm, mla_w_ukv, diff_lq1, diff_lk1,
           diff_lq2, diff_lk2, diff_subln, conv_w, conv_b, conv_ln_g, conv_ln_b, w_branch, w_out, ln1_g, ln1_b,
           ln2_g, ln2_b, router_w, router_bias, exp_w1, exp_w3, exp_w2):
    B, S, D = x.shape
    C = ctx.shape[1]
    depth = w_in.shape[0]
    N, NC = B * S, B * C
    tm = ROW_TILE
    assert D == D_MODEL and S % tm == 0 and NC % tm == 0 and S % C == 0 and S % GRID_W == 0

    R = N + NC

    def seg_of_row(r):
        return jnp.minimum(r // S, B)

    def seg(i):
        return seg_of_row(i * tm)

    def pos_block(i):
        return jnp.where(i < N // tm, i % (S // tm), S // tm)

    rows = S // GRID_W
    tabs_mq = _rope_tables(rows, MLA_DR, MLA_DN, HEAD_PAD, MLA_HEADS, tm)
    tabs_mk = _rope_tables(rows, MLA_DR, 0, LANES, 1, tm)
    tabs_d = _rope_tables(rows, DIFF_D, 0, DIFF_D, 2 * DIFF_HEADS, tm)

    nseg = 8
    cond = jnp.zeros((nseg, D), F32).at[:B].set(c).at[B].set(c_ctx)
    mod_all = _ada(cond, w_ada, b_ada).reshape(depth, nseg * 6, 1, D)

    dft_ch = _dft_channel_matrix().astype(BF16)
    dft_ctx = _dft_position_matrix(C).astype(BF16)
    if S % (FFT_MINOR * FFT_GROUP) == 0:
        fnet_lat = lambda y: _fnet_positions_fft(y, B, S)
    else:
        dft_lat = _dft_position_matrix(S).astype(BF16)
        fnet_lat = lambda y: _fnet_positions(y, dft_lat, 0, B, S)
    rw_t = router_w.T
    rw_hi = rw_t.astype(BF16)
    rw_lo = (rw_t - rw_hi.astype(F32)).astype(BF16)
    router = (rw_hi, rw_lo, router_bias.reshape(N_EXPERTS, 1))

    xs = jnp.concatenate([x.reshape(N, D), ctx.reshape(NC, D)], axis=0)
    alpha = (2.0 * depth) ** 0.25
    xs, h = _norm(xs, seg=seg, alpha=alpha, first=True, mod_h=mod_all[0], k_scale=1, k_shift=0, name="norm_in")

    for l in range(depth):
        mod = mod_all[l]
        wl = w_in[l]
        w_mla = jnp.pad(wl[:, :OFF_DIFF], ((0, 0), (0, MLA_COLS_PAD - MLA_COLS))).astype(BF16)
        cm = _project(h, w_mla, F32, "proj_mla", MLA_COLS_PAD)
        cd = _project(h, wl[:, OFF_DIFF:OFF_CONV].astype(BF16), BF16, "proj_diff", DIFF_COLS)
        cc = _project(h, wl[:, OFF_CONV:OFF_FNET].astype(BF16), BF16, "proj_conv", CONV_COLS)
        cf = _project(h, wl[:, OFF_FNET:OFF_GATE].astype(BF16), BF16, "proj_fnet", FNET_COLS)
        gates = _project(h, wl[:, OFF_GATE:].astype(BF16), BF16, "proj_gate", 2048)

        wq = _pad_heads(mla_w_uq[l], MLA_HEADS, MLA_DN + MLA_DR, 0, MLA_DN + MLA_DR).astype(BF16)
        wk = _pad_heads(mla_w_ukv[l], MLA_HEADS, MLA_DN + MLA_DV, 0, MLA_DN).astype(BF16)
        wv = _pad_heads(mla_w_ukv[l], MLA_HEADS, MLA_DN + MLA_DV, MLA_DN, MLA_DN + MLA_DV).astype(BF16)
        mq, mk, mv = _mla_prep(cm, mla_q_norm[l].reshape(1, -1), mla_kv_norm[l].reshape(1, -1), wq, wk, wv,
                               tabs_mq, tabs_mk, pos_block)
        o_mla = jnp.concatenate([_attention("mla", mq, mk, mv, B=B, S=S, C=C, ctx_queries=False),
                                 _attention("mla", mq, mk, mv, B=B, S=S, C=C, ctx_queries=True)], axis=0)

        lam_init = 0.8 - 0.6 * math.exp(-0.3 * l)
        ll = jnp.stack([diff_lq1[l], diff_lk1[l], diff_lq2[l], diff_lk2[l]], axis=0)
        sub = diff_subln[l].reshape(1, -1)
        dq, dk, dv = _diff_prep(cd, tabs_d, pos_block)
        o_diff = jnp.concatenate(
            [_attention("diff", dq, dk, dv, B=B, S=S, C=C, ctx_queries=False, extra=(ll, sub), lam_init=lam_init),
             _attention("diff", dq, dk, dv, B=B, S=S, C=C, ctx_queries=True, extra=(ll, sub), lam_init=lam_init)],
            axis=0)

        o_conv = jnp.concatenate([_conv(cc, 0, B, S, conv_w[l], conv_b[l], conv_ln_g[l], conv_ln_b[l]),
                                  _conv(cc, N, B, C, conv_w[l], conv_b[l], conv_ln_g[l], conv_ln_b[l])], axis=0)
        y = _project(cf, dft_ch, BF16, "fnet_ch", 2 * FNET_COLS)
        o_fnet = jnp.concatenate([fnet_lat(y), _fnet_positions(y, dft_ctx, N, B, C)], axis=0)

        mix = _merge([o_mla, o_diff, o_conv, o_fnet], gates, w_branch[l].astype(BF16), w_out[l].astype(BF16))
        xs, h2, rec, counts = _norm(xs, seg=seg, alpha=alpha, m=mix, mod_g=mod, k_gate=2, ln_g=ln1_g[l],
                                    ln_b=ln1_b[l], mod_h=mod, k_scale=4, k_shift=3, router=router, name="norm1")
        pos, wgt, tile_expert, n_used, tail = _moe_plan(rec, counts, R)
        sorted_rows = _moe_dispatch(h2, pos, tail, tile_expert.shape[0] * MOE_TILE)
        ys = _moe_experts(sorted_rows, tile_expert, n_used, exp_w1[l].astype(BF16), exp_w3[l].astype(BF16),
                          exp_w2[l].astype(BF16))
        last = l + 1 == depth
        out = _moe_combine_norm(xs, ys, pos, wgt, seg_of_row=seg_of_row, alpha=alpha, mod_g=mod, k_gate=5,
                                ln_g=ln2_g[l], ln_b=ln2_b[l], mod_h=None if last else mod_all[l + 1],
                                k_scale=1, k_shift=0, name="norm_out" if last else "norm2")
        if last:
            (xs,) = out
        else:
            xs, h = out
    return xs[:N].reshape(B, S, D)
```

```python
import functools
import math

import jax
import jax.numpy as jnp
from jax import lax
from jax.experimental import pallas as pl
from jax.experimental.pallas import tpu as pltpu

F32 = jnp.float32
BF16 = jnp.bfloat16

D_MODEL = 2048
GRID_W = 64
ROPE_THETA = 10000.0
EPS = 1e-6

MLA_HEADS = 8
MLA_DN = 64
MLA_DR = 32
MLA_DV = 64
MLA_Q_RANK = 384
MLA_KV_RANK = 256
DIFF_HEADS = 4
DIFF_D = 64
CONV_W = 512
CONV_K = 31
FNET_GROUPS = 4
FNET_GROUP_W = 128
N_BRANCH = 4
BRANCH_W = 512
N_EXPERTS = 16
N_GROUPS = 4
EXPERTS_PER_GROUP = N_EXPERTS // N_GROUPS
D_FF = 1024

MLA_COLS = MLA_Q_RANK + MLA_KV_RANK + MLA_DR
DIFF_COLS = 3 * DIFF_HEADS * 2 * DIFF_D
CONV_COLS = 2 * CONV_W
FNET_COLS = FNET_GROUPS * FNET_GROUP_W
GATE_COLS = N_BRANCH * D_MODEL
OFF_DIFF = MLA_COLS
OFF_CONV = OFF_DIFF + DIFF_COLS
OFF_FNET = OFF_CONV + CONV_COLS
OFF_GATE = OFF_FNET + FNET_COLS

LANES = 128
SUBLANES = 8
HEAD_PAD = 128
MLA_COLS_PAD = 768
CONV_HALO = 16
VMEM_LIMIT = 56 * 1024 * 1024
LOG2E = math.log2(math.e)

ROW_TILE = 512
MERGE_TILE = 256
MOE_TILE = 512
COMBINE_TILE = 256
ATTN_TQ = 512
ATTN_TK = 1024
ATTN_UNROLL = 8


def _params(*sem):
    return pltpu.CompilerParams(dimension_semantics=("arbitrary",) * len(sem), vmem_limit_bytes=VMEM_LIMIT)


def _mm_kernel(a_ref, w_ref, o_ref, acc_ref, *, nk):
    prod = jnp.dot(a_ref[...], w_ref[...], preferred_element_type=F32)
    if nk == 1:
        o_ref[...] = prod.astype(o_ref.dtype)
        return
    k = pl.program_id(3)

    @pl.when(k == 0)
    def _():
        acc_ref[...] = prod

    @pl.when(k > 0)
    def _():
        acc_ref[...] += prod

    @pl.when(k == nk - 1)
    def _():
        o_ref[...] = acc_ref[...].astype(o_ref.dtype)


def _matmul(a, w, *, out_shape, out_dtype, tm, tn, tk, grid, a_map, o_map, name, w_map=None, w_spec=None):
    nk = grid[3]
    if w_spec is None:
        w_spec = pl.BlockSpec((tk, tn), w_map)
    return pl.pallas_call(
        functools.partial(_mm_kernel, nk=nk),
        grid=grid,
        in_specs=[pl.BlockSpec((tm, tk), a_map), w_spec],
        out_specs=pl.BlockSpec((tm, tn), o_map),
        out_shape=jax.ShapeDtypeStruct(out_shape, out_dtype),
        scratch_shapes=[pltpu.VMEM((tm, tn), F32)],
        compiler_params=_params("parallel", "parallel", "parallel", "arbitrary"),
        name=name,
    )(a, w)


def _project(h, w, out_dtype, name, tn, layer=None):
    R, K = h.shape
    N = w.shape[-1]
    tm = ROW_TILE
    if layer is None:
        w_spec = pl.BlockSpec((K, tn), lambda i, j, u, k: (0, j))
    else:
        w_spec = pl.BlockSpec((None, K, tn), lambda i, j, u, k: (layer, 0, j))
    return _matmul(h, w, out_shape=(R, N), out_dtype=out_dtype, tm=tm, tn=tn, tk=K,
                   grid=(R // tm, N // tn, 1, 1),
                   a_map=lambda i, j, u, k: (i, 0), w_spec=w_spec,
                   o_map=lambda i, j, u, k: (i, j), name=name)


def _ada_kernel(c_ref, w_ref, b_ref, o_ref):
    c = c_ref[...]
    a = (c * jax.nn.sigmoid(c)).astype(BF16)
    o_ref[...] = jnp.dot(a, w_ref[...].astype(BF16), preferred_element_type=F32) + b_ref[...]


def _ada(cond, w_ada, b_ada):
    L, D, N6 = w_ada.shape
    rows = cond.shape[0]
    tn = 1024
    return pl.pallas_call(
        _ada_kernel,
        grid=(L, N6 // tn),
        in_specs=[pl.BlockSpec((rows, D), lambda l, j: (0, 0)),
                  pl.BlockSpec((None, D, tn), lambda l, j: (l, 0, j)),
                  pl.BlockSpec((None, 1, tn), lambda l, j: (l, 0, j))],
        out_specs=pl.BlockSpec((None, rows, tn), lambda l, j: (l, 0, j)),
        out_shape=jax.ShapeDtypeStruct((L, rows, N6), F32),
        compiler_params=_params("parallel", "parallel"),
        name="ada_mod",
    )(cond, w_ada, b_ada.reshape(L, 1, N6))


def _route(logits_t, bias_t):
    scores = jax.nn.sigmoid(logits_t)
    sel = scores + bias_t
    sel_r = [sel[e:e + 1, :] for e in range(N_EXPERTS)]
    sc_r = [scores[e:e + 1, :] for e in range(N_EXPERTS)]
    per = EXPERTS_PER_GROUP
    gscore = []
    for g in range(N_GROUPS):
        r = sel_r[g * per:(g + 1) * per]
        best = None
        for a in range(per):
            for b in range(a + 1, per):
                s = r[a] + r[b]
                best = s if best is None else jnp.maximum(best, s)
        gscore.append(best)
    best_g = jnp.zeros_like(gscore[0], dtype=jnp.int32)
    best_v = gscore[0]
    for g in range(1, N_GROUPS):
        better = gscore[g] > best_v
        best_g = jnp.where(better, g, best_g)
        best_v = jnp.where(better, gscore[g], best_v)
    v = []
    s = []
    for j in range(per):
        vj = sel_r[j]
        sj = sc_r[j]
        for g in range(1, N_GROUPS):
            vj = jnp.where(best_g == g, sel_r[g * per + j], vj)
            sj = jnp.where(best_g == g, sc_r[g * per + j], sj)
        v.append(vj)
        s.append(sj)
    chosen = []
    for j in range(per):
        rank = jnp.zeros_like(best_g)
        for i in range(per):
            if i == j:
                continue
            ahead = (v[i] > v[j]) | ((v[i] == v[j]) & (i < j))
            rank = rank + ahead.astype(jnp.int32)
        chosen.append(rank < 2)
    total = jnp.zeros_like(s[0])
    for j in range(per):
        total = total + jnp.where(chosen[j], s[j], 0.0)
    picked, weight = [], []
    for e in range(N_EXPERTS):
        g, j = divmod(e, per)
        on = (best_g == g) & chosen[j]
        picked.append(on)
        weight.append(jnp.where(on, s[j] / total, 0.0))
    return picked, weight


def _dispatch_info(picked, weight, tri_ref, tot_ref):
    sel = jnp.concatenate([p.astype(F32) for p in picked], axis=0)
    cum = jnp.dot(sel.astype(BF16), tri_ref[...], preferred_element_type=F32)
    tot = tot_ref[...]
    rank = cum + tot[:, 0:1]
    tot_ref[...] = tot + jnp.sum(sel, axis=1, keepdims=True)
    zero = jnp.zeros_like(weight[0])
    seen = zero
    rec = [zero] * 6
    for e in range(N_EXPERTS):
        on = picked[e]
        for slot, is_slot in ((0, on & (seen == 0.0)), (1, on & (seen == 1.0))):
            rec[slot] = jnp.where(is_slot, float(e), rec[slot])
            rec[2 + slot] = jnp.where(is_slot, rank[e:e + 1, :], rec[2 + slot])
            rec[4 + slot] = jnp.where(is_slot, weight[e], rec[4 + slot])
        seen = seen + on.astype(F32)
    return jnp.concatenate(rec + [zero, zero], axis=0)


def _standardize(y):
    mu = jnp.mean(y, axis=-1, keepdims=True)
    d = y - mu
    var = jnp.mean(d * d, axis=-1, keepdims=True)
    return d * lax.rsqrt(var + EPS)


def _norm_kernel(*refs, first, emit_h, router, alpha):
    refs = list(refs)
    x_ref = refs.pop(0)
    if not first:
        m_ref, g_ref, lng_ref, lnb_ref = refs[:4]
        refs = refs[4:]
    if emit_h:
        sc_ref, sh_ref = refs[:2]
        refs = refs[2:]
    if router:
        rwh_ref, rwl_ref, rb_ref, tri_ref = refs[:4]
        refs = refs[4:]
    xo_ref = refs.pop(0)
    x = x_ref[...]
    xn = _standardize(x if first else alpha * x + g_ref[...] * m_ref[...])
    if not first:
        xn = xn * lng_ref[...] + lnb_ref[...]
    xo_ref[...] = xn
    if not emit_h:
        return
    h_ref = refs.pop(0)
    h = xn * (1.0 + sc_ref[...]) + sh_ref[...]
    h_ref[...] = h.astype(h_ref.dtype)
    if router:
        rec_ref, cnt_ref, tot_ref = refs

        @pl.when(pl.program_id(0) == 0)
        def _():
            tot_ref[...] = jnp.zeros_like(tot_ref)

        h_hi = h.astype(BF16)
        h_lo = (h - h_hi.astype(F32)).astype(BF16)
        nt = (((1,), (1,)), ((), ()))
        logits_t = (lax.dot_general(rwh_ref[...], h_hi, nt, preferred_element_type=F32)
                    + lax.dot_general(rwh_ref[...], h_lo, nt, preferred_element_type=F32)
                    + lax.dot_general(rwl_ref[...], h_hi, nt, preferred_element_type=F32))
        picked, weight = _route(logits_t, rb_ref[...])
        rec_ref[...] = _dispatch_info(picked, weight, tri_ref, tot_ref)
        cnt_ref[...] = tot_ref[...]


def _norm(x, *, seg, alpha, first=False, m=None, mod_g=None, k_gate=None, ln_g=None, ln_b=None,
          mod_h=None, k_scale=None, k_shift=None, router=None, name):
    R, D = x.shape
    tm = ROW_TILE
    emit_h = mod_h is not None
    row = pl.BlockSpec((tm, D), lambda i: (i, 0))

    def mod_spec(k):
        return pl.BlockSpec((None, 1, D), lambda i: (seg(i) * 6 + k, 0, 0))

    vec = pl.BlockSpec((1, D), lambda i: (0, 0))
    args, specs = [x], [row]
    if not first:
        args += [m, mod_g, ln_g.reshape(1, D), ln_b.reshape(1, D)]
        specs += [row, mod_spec(k_gate), vec, vec]
    if emit_h:
        args += [mod_h, mod_h]
        specs += [mod_spec(k_scale), mod_spec(k_shift)]
    scratch = []
    if router is not None:
        rwh, rwl, rb = router
        tri = (jnp.arange(tm)[:, None] <= jnp.arange(tm)[None, :]).astype(BF16)
        args += [rwh, rwl, rb, tri]
        specs += [pl.BlockSpec((N_EXPERTS, D), lambda i: (0, 0))] * 2
        specs += [pl.BlockSpec((N_EXPERTS, 1), lambda i: (0, 0)), pl.BlockSpec((tm, tm), lambda i: (0, 0))]
        scratch = [pltpu.VMEM((N_EXPERTS, LANES), F32)]
    out_shape = [jax.ShapeDtypeStruct((R, D), F32)]
    out_specs = [row]
    if emit_h:
        out_shape.append(jax.ShapeDtypeStruct((R, D), BF16 if router is None else F32))
        out_specs.append(row)
    if router is not None:
        out_shape += [jax.ShapeDtypeStruct((8, R), F32), jax.ShapeDtypeStruct((N_EXPERTS, LANES), F32)]
        out_specs += [pl.BlockSpec((8, tm), lambda i: (0, i)), pl.BlockSpec((N_EXPERTS, LANES), lambda i: (0, 0))]
    return pl.pallas_call(
        functools.partial(_norm_kernel, first=first, emit_h=emit_h, router=router is not None, alpha=alpha),
        grid=(R // tm,),
        in_specs=specs,
        out_specs=out_specs,
        out_shape=out_shape,
        scratch_shapes=scratch,
        compiler_params=_params("arbitrary" if router is not None else "parallel"),
        name=name,
    )(*args)


def _rope_tables(rows, rot_dim, lane0, chunk, reps, ident_rows):
    n = rows * GRID_W
    r = jnp.broadcast_to(jnp.arange(rows, dtype=F32)[:, None], (rows, GRID_W)).reshape(n)
    c = jnp.broadcast_to(jnp.arange(GRID_W, dtype=F32)[None, :], (rows, GRID_W)).reshape(n)
    quarter = rot_dim // 4
    inv_freq = ROPE_THETA ** (-jnp.arange(quarter, dtype=F32) / quarter)
    ar = r[:, None] * inv_freq
    ac = c[:, None] * inv_freq
    ang = jnp.concatenate([ar, ar, ac, ac], -1)
    sign = jnp.concatenate([-jnp.ones((quarter,), F32), jnp.ones((quarter,), F32)] * 2)
    cos = jnp.ones((n, chunk), F32).at[:, lane0:lane0 + rot_dim].set(jnp.cos(ang))
    sin = jnp.zeros((n, chunk), F32).at[:, lane0:lane0 + rot_dim].set(jnp.sin(ang) * sign)
    cos = jnp.concatenate([cos, jnp.ones((ident_rows, chunk), F32)], 0)
    sin = jnp.concatenate([sin, jnp.zeros((ident_rows, chunk), F32)], 0)
    return jnp.tile(cos, (1, reps)), jnp.tile(sin, (1, reps))


def _rope(x, cos, sin_signed, quarter, lane0):
    w = x.shape[-1]
    lane = lax.broadcasted_iota(jnp.int32, x.shape, 1)
    even = (((lane - lane0) // quarter) % 2) == 0
    partner = jnp.where(even, pltpu.roll(x, w - quarter, 1), pltpu.roll(x, quarter, 1))
    return x * cos + partner * sin_signed


def _rms(x, g):
    return x * lax.rsqrt(jnp.mean(x * x, axis=-1, keepdims=True) + EPS) * g


def _mla_prep_kernel(cm_ref, qn_ref, kvn_ref, wq_ref, wk_ref, wv_ref, cq_ref, sq_ref, ck_ref, sk_ref,
                     q_ref, k_ref, v_ref):
    cm = cm_ref[...]
    cq = cm[:, :MLA_Q_RANK]
    ckv = cm[:, MLA_Q_RANK:MLA_Q_RANK + MLA_KV_RANK]
    kr = cm[:, MLA_Q_RANK + MLA_KV_RANK:]
    qn = _rms(cq, qn_ref[...]).astype(BF16)
    kvn = _rms(ckv, kvn_ref[...]).astype(BF16)
    q = jnp.dot(qn, wq_ref[...], preferred_element_type=F32)
    q = _rope(q, cq_ref[...], sq_ref[...], MLA_DR // 4, MLA_DN)
    scale = (MLA_DN + MLA_DR) ** -0.5 * LOG2E
    q_ref[...] = (q * scale).astype(BF16)
    kr = _rope(kr, ck_ref[...], sk_ref[...], MLA_DR // 4, 0)
    kr = pltpu.roll(kr, MLA_DN, 1)
    k = jnp.dot(kvn, wk_ref[...], preferred_element_type=F32)
    k_ref[...] = (k + jnp.concatenate([kr] * MLA_HEADS, axis=1)).astype(BF16)
    v = jnp.dot(kvn, wv_ref[...], preferred_element_type=F32)
    lane = lax.broadcasted_iota(jnp.int32, v.shape, 1)
    v_ref[...] = jnp.where(lane % HEAD_PAD == MLA_DV, 1.0, v).astype(BF16)


def _mla_prep(cm, q_norm, kv_norm, wq, wk, wv, tabs_q, tabs_k, pos_block):
    R = cm.shape[0]
    tm = ROW_TILE
    W = MLA_HEADS * HEAD_PAD
    row = lambda w: pl.BlockSpec((tm, w), lambda i: (i, 0))
    full = lambda a: pl.BlockSpec(a.shape, lambda i: (0, 0))
    tab = lambda w: pl.BlockSpec((tm, w), lambda i: (pos_block(i), 0))
    outs = pl.pallas_call(
        _mla_prep_kernel,
        grid=(R // tm,),
        in_specs=[row(MLA_COLS_PAD), full(q_norm), full(kv_norm), full(wq), full(wk), full(wv),
                  tab(W), tab(W), tab(LANES), tab(LANES)],
        out_specs=[row(W)] * 3,
        out_shape=[jax.ShapeDtypeStruct((R, W), BF16)] * 3,
        compiler_params=_params("parallel"),
        name="mla_prep",
    )(cm, q_norm, kv_norm, wq, wk, wv, tabs_q[0], tabs_q[1], tabs_k[0], tabs_k[1])
    return outs


def _diff_prep_kernel(cd_ref, cos_ref, sin_ref, q_ref, k_ref, v_ref):
    w = DIFF_HEADS * 2 * DIFF_D
    cd = cd_ref[...].astype(F32)
    cos = cos_ref[...]
    sin = sin_ref[...]
    q = _rope(cd[:, :w], cos, sin, DIFF_D // 4, 0)
    q_ref[...] = (q * (DIFF_D ** -0.5 * LOG2E)).astype(BF16)
    k_ref[...] = _rope(cd[:, w:2 * w], cos, sin, DIFF_D // 4, 0).astype(BF16)
    v = cd_ref[:, 2 * w:]
    lane = lax.broadcasted_iota(jnp.int32, (v.shape[0], LANES), 1)
    ones = jnp.where(lane == 0, 1.0, 0.0).astype(BF16)
    parts = []
    for h in range(DIFF_HEADS):
        parts += [v[:, h * 2 * DIFF_D:(h + 1) * 2 * DIFF_D], ones]
    v_ref[...] = jnp.concatenate(parts, axis=1)


def _diff_prep(cd, tabs, pos_block):
    R = cd.shape[0]
    tm = ROW_TILE
    w = DIFF_HEADS * 2 * DIFF_D
    row = lambda c: pl.BlockSpec((tm, c), lambda i: (i, 0))
    tab = pl.BlockSpec((tm, w), lambda i: (pos_block(i), 0))
    return pl.pallas_call(
        _diff_prep_kernel,
        grid=(R // tm,),
        in_specs=[row(DIFF_COLS), tab, tab],
        out_specs=[row(w), row(w), row(2 * w)],
        out_shape=[jax.ShapeDtypeStruct((R, w), BF16), jax.ShapeDtypeStruct((R, w), BF16),
                   jax.ShapeDtypeStruct((R, 2 * w), BF16)],
        compiler_params=_params("parallel"),
        name="diff_prep",
    )(cd, tabs[0], tabs[1])


def _flash(streams, sources, vw, unroll):
    tq = streams[0][0].shape[0]
    carry = tuple((jnp.full((tq, 1), -jnp.inf, F32), jnp.zeros((tq, vw), F32)) for _ in streams)
    for k_ref, v_ref, length, tk in sources:
        def step(c, carry, k_ref=k_ref, v_ref=v_ref, tk=tk):
            r0 = pl.multiple_of(c * tk, tk)
            out = []
            for (q, kl0, vl0), (m, acc) in zip(streams, carry):
                k = k_ref[pl.ds(r0, tk), kl0:kl0 + HEAD_PAD]
                v = v_ref[pl.ds(r0, tk), vl0:vl0 + vw]
                s = lax.dot_general(q, k, (((1,), (1,)), ((), ())), preferred_element_type=F32)
                m_new = jnp.maximum(m, jnp.max(s, axis=1, keepdims=True))
                p = jnp.exp2(s - m_new)
                a = jnp.exp2(m - m_new)
                acc = a * acc + jnp.dot(p.astype(BF16), v, preferred_element_type=F32)
                out.append((m_new, acc))
            return tuple(out)
        n = length // tk
        if n == 1:
            carry = step(0, carry)
        else:
            carry = lax.fori_loop(0, n, step, carry, unroll=unroll)
    return [acc for _, acc in carry]


def _mla_attn_kernel(*refs, n_src, lens, tk, unroll):
    q_ref = refs[0]
    kv = refs[1:1 + 2 * n_src]
    o_ref = refs[1 + 2 * n_src]
    streams = [(q_ref[:, hh * HEAD_PAD:(hh + 1) * HEAD_PAD], hh * HEAD_PAD, hh * HEAD_PAD) for hh in range(2)]
    sources = [(kv[2 * s], kv[2 * s + 1], lens[s], min(tk, lens[s])) for s in range(n_src)]
    accs = _flash(streams, sources, HEAD_PAD, unroll)
    outs = [acc[:, :MLA_DV] / acc[:, MLA_DV:MLA_DV + 1] for acc in accs]
    o_ref[...] = jnp.concatenate(outs, axis=1).astype(BF16)


def _diff_attn_kernel(*refs, n_src, lens, tk, unroll, lam_init):
    q_ref, ll_ref, sub_ref = refs[:3]
    kv = refs[3:3 + 2 * n_src]
    o_ref = refs[3 + 2 * n_src]
    ll = ll_ref[...]
    lam = (jnp.exp(jnp.sum(ll[0:1, :] * ll[1:2, :], axis=1, keepdims=True))
           - jnp.exp(jnp.sum(ll[2:3, :] * ll[3:4, :], axis=1, keepdims=True)) + lam_init)
    q = q_ref[...]
    lane = lax.broadcasted_iota(jnp.int32, q.shape, 1)
    vw = 2 * HEAD_PAD
    dv = 2 * DIFF_D
    streams = [(jnp.where((lane // DIFF_D) == mp, q, jnp.zeros_like(q)), 0, 0) for mp in range(2)]
    sources = [(kv[2 * s], kv[2 * s + 1], lens[s], min(tk, lens[s])) for s in range(n_src)]
    accs = _flash(streams, sources, vw, unroll)
    o = [acc[:, :dv] / acc[:, dv:dv + 1] for acc in accs]
    od = o[0] - lam * o[1]
    o_ref[...] = (_rms(od, sub_ref[...]) * (1.0 - lam_init)).astype(BF16)


def _attention(kind, q, k, v, *, B, S, C, ctx_queries, extra=None, lam_init=None):
    N = B * S
    tk = ATTN_TK
    unroll = ATTN_UNROLL
    if kind == "mla":
        steps, qw, kw, vw, ow = MLA_HEADS // 2, 2 * HEAD_PAD, 2 * HEAD_PAD, 2 * HEAD_PAD, 2 * MLA_DV
    else:
        steps, qw, kw, vw, ow = DIFF_HEADS, HEAD_PAD, HEAD_PAD, 2 * HEAD_PAD, 2 * DIFF_D
    cblk = N // C
    if ctx_queries:
        tq, nq, rows_out = C, 1, B * C
        q_spec = pl.BlockSpec((tq, qw), lambda b, h, i: (cblk + b, h))
        kv_specs = [pl.BlockSpec((C, kw), lambda b, h, i: (cblk + b, h)),
                    pl.BlockSpec((C, vw), lambda b, h, i: (cblk + b, h))]
        lens = (C,)
    else:
        tq = ATTN_TQ
        nq, rows_out = S // tq, N
        q_spec = pl.BlockSpec((tq, qw), lambda b, h, i: (b * nq + i, h))
        kv_specs = [pl.BlockSpec((S, kw), lambda b, h, i: (b, h)),
                    pl.BlockSpec((S, vw), lambda b, h, i: (b, h)),
                    pl.BlockSpec((C, kw), lambda b, h, i: (cblk + b, h)),
                    pl.BlockSpec((C, vw), lambda b, h, i: (cblk + b, h))]
        lens = (S, C)
    n_src = len(lens)
    kv_args = [k, v] * n_src
    o_spec = pl.BlockSpec((tq, ow), lambda b, h, i: (b * nq + i, h))
    if kind == "mla":
        body = functools.partial(_mla_attn_kernel, n_src=n_src, lens=lens, tk=tk, unroll=unroll)
        args, specs = [q], [q_spec]
    else:
        ll, sub = extra
        body = functools.partial(_diff_attn_kernel, n_src=n_src, lens=lens, tk=tk, unroll=unroll, lam_init=lam_init)
        args = [q, ll, sub]
        specs = [q_spec, pl.BlockSpec(ll.shape, lambda b, h, i: (0, 0)), pl.BlockSpec(sub.shape, lambda b, h, i: (0, 0))]
    return pl.pallas_call(
        body,
        grid=(B, steps, nq),
        in_specs=specs + kv_specs,
        out_specs=o_spec,
        out_shape=jax.ShapeDtypeStruct((rows_out, steps * ow), BF16),
        compiler_params=_params("parallel", "parallel", "parallel"),
        name=f"{kind}_attn_{'ctx' if ctx_queries else 'lat'}",
    )(*args, *kv_args)


def _conv_kernel(main_ref, prev_ref, next_ref, w_ref, b_ref, g_ref, bb_ref, o_ref, z_ref, sh_ref, *, ts, nt):
    i = pl.program_id(1)

    def glu(blk):
        a = blk[:, :CONV_W].astype(F32)
        g = blk[:, CONV_W:].astype(F32)
        return a * jax.nn.sigmoid(g)

    H = CONV_HALO
    z_ref[0:H, :] = glu(prev_ref[...]) * (i > 0).astype(F32)
    z_ref[H:H + ts, :] = glu(main_ref[...])
    z_ref[H + ts:2 * H + ts, :] = glu(next_ref[...]) * (i < nt - 1).astype(F32)
    span = sh_ref.shape[1]
    for s in range(1, SUBLANES):
        sh_ref[s] = z_ref[s:s + span, :]
    rc = 32
    first = H - CONV_K // 2
    for r in range(ts // rc):
        acc = jnp.broadcast_to(b_ref[...], (rc, CONV_W))
        for k in range(CONV_K):
            s, a = (k + first) % SUBLANES, (k + first) // SUBLANES * SUBLANES
            src = z_ref[r * rc + a:r * rc + a + rc, :] if s == 0 else sh_ref[s, r * rc + a:r * rc + a + rc, :]
            acc = acc + w_ref[k:k + 1, :] * src
        mu = jnp.mean(acc, axis=-1, keepdims=True)
        d = acc - mu
        var = jnp.mean(d * d, axis=-1, keepdims=True)
        y = d * lax.rsqrt(var + EPS) * g_ref[...] + bb_ref[...]
        o_ref[r * rc:(r + 1) * rc, :] = (y * jax.nn.sigmoid(y)).astype(BF16)


def _conv(cols, row_off, nb, L, conv_w, conv_b, ln_g, ln_b):
    R = cols.shape[0]
    ts = min(ROW_TILE, L)
    nt = L // ts
    H = CONV_HALO
    hb = ts // H
    base = lambda b, i: (row_off + b * L + i * ts) // H
    vec = lambda a: pl.BlockSpec(a.shape, lambda b, i: (0, 0))
    cb, g, bb = conv_b.reshape(1, CONV_W), ln_g.reshape(1, CONV_W), ln_b.reshape(1, CONV_W)
    return pl.pallas_call(
        functools.partial(_conv_kernel, ts=ts, nt=nt),
        grid=(nb, nt),
        in_specs=[pl.BlockSpec((ts, CONV_COLS), lambda b, i: ((row_off + b * L) // ts + i, 0)),
                  pl.BlockSpec((H, CONV_COLS), lambda b, i: (jnp.maximum(base(b, i) - 1, 0), 0)),
                  pl.BlockSpec((H, CONV_COLS), lambda b, i: (jnp.minimum(base(b, i) + hb, R // H - 1), 0)),
                  vec(conv_w), vec(cb), vec(g), vec(bb)],
        out_specs=pl.BlockSpec((ts, CONV_W), lambda b, i: (b * nt + i, 0)),
        out_shape=jax.ShapeDtypeStruct((nb * L, CONV_W), BF16),
        scratch_shapes=[pltpu.VMEM((ts + 2 * H, CONV_W), F32),
                        pltpu.VMEM((SUBLANES, ts + 2 * H - SUBLANES, CONV_W), F32)],
        compiler_params=_params("parallel", "parallel"),
        name=f"conv_{L}",
    )(cols, cols, cols, conv_w, cb, g, bb)


def _dft_channel_matrix():
    j = jnp.arange(FNET_GROUP_W, dtype=jnp.int32)
    ang = (2.0 * math.pi / FNET_GROUP_W) * ((j[:, None] * j[None, :]) % FNET_GROUP_W).astype(F32)
    eye = jnp.eye(FNET_GROUPS, dtype=F32)
    scale = FNET_GROUP_W ** -0.5
    return jnp.concatenate([jnp.kron(eye, jnp.cos(ang)), jnp.kron(eye, jnp.sin(ang))], axis=1) * scale


def _dft_position_matrix(L):
    n = jnp.arange(L, dtype=jnp.int32)
    ang = (2.0 * math.pi / L) * ((n[:, None] * n[None, :]) % L).astype(F32)
    return jnp.concatenate([jnp.cos(ang), -jnp.sin(ang)], axis=1) * (L ** -0.5)


def _fnet_positions(y, a, row_off, nb, L):
    tm = min(ROW_TILE, L)
    tk = min(2048, L)
    kb = L // tk
    off = row_off // tk
    return _matmul(a, y, out_shape=(nb * L, FNET_COLS), out_dtype=F32, tm=tm, tn=FNET_COLS, tk=tk,
                   grid=(nb, L // tm, 1, 2 * kb),
                   a_map=lambda b, i, u, k: (i, k),
                   w_map=lambda b, i, u, k: (off + b * kb + k % kb, k // kb),
                   o_map=lambda b, i, u, k: (b * (L // tm) + i, 0),
                   name=f"fnet_pos_{L}")


FFT_MINOR = 128
FFT_GROUP = 8
FFT_COLS = 8


def _fft_major_matrix(n1):
    k = jnp.arange(n1, dtype=jnp.int32)
    ang = (2.0 * math.pi / n1) * ((k[:, None] * k[None, :]) % n1).astype(F32)
    return jnp.concatenate([jnp.cos(ang), jnp.sin(ang)], axis=0).astype(BF16)


def _fft_minor_matrices(L):
    n2c, grp = FFT_MINOR, FFT_GROUP
    n1 = L // n2c
    g = jnp.arange(n1 // grp, dtype=jnp.int32)[:, None, None, None, None]
    k2 = jnp.arange(n2c, dtype=jnp.int32)[None, :, None, None, None]
    j = jnp.arange(grp, dtype=jnp.int32)[None, None, :, None, None]
    jp = jnp.arange(grp, dtype=jnp.int32)[None, None, None, :, None]
    n2 = jnp.arange(n2c, dtype=jnp.int32)[None, None, None, None, :]
    phase = (n2 * k2 * n1 + n2 * (grp * g + j)) % L
    ang = (2.0 * math.pi / L) * phase.astype(F32)
    keep = (j == jp).astype(F32) * (L ** -0.5)
    shape = (n1 // grp, n2c * grp, grp * n2c)
    return ((jnp.cos(ang) * keep).reshape(shape).astype(BF16), (jnp.sin(ang) * keep).reshape(shape).astype(BF16))


def _fft_major_kernel(cs_ref, y_ref, a_ref, *, n1):
    pq = jnp.dot(cs_ref[...], y_ref[...], preferred_element_type=F32)
    p, q = pq[:n1], pq[n1:]
    w = FNET_COLS
    parts = []
    for t in range(FFT_COLS):
        re, im = slice(2 * t * w, (2 * t + 1) * w), slice((2 * t + 1) * w, (2 * t + 2) * w)
        parts += [p[:, re] - q[:, im], -(p[:, im] + q[:, re])]
    a_ref[...] = jnp.concatenate(parts, axis=1).astype(BF16)


def _fft_minor_kernel(mc_ref, ms_ref, a_ref, o_ref):
    w = FNET_COLS
    out = (jnp.dot(mc_ref[...], a_ref[:, :w], preferred_element_type=F32)
           + jnp.dot(ms_ref[...], a_ref[:, w:], preferred_element_type=F32))
    o_ref[...] = out.reshape(o_ref.shape)


def _fnet_positions_fft(y, nb, L):
    n2c, grp = FFT_MINOR, FFT_GROUP
    n1 = L // n2c
    w2 = 2 * FNET_COLS
    cs = _fft_major_matrix(n1)
    mc, ms = _fft_minor_matrices(L)
    flat = y[:nb * L].reshape(nb * n1, n2c * w2)
    tc = FFT_COLS * w2
    a = pl.pallas_call(
        functools.partial(_fft_major_kernel, n1=n1),
        grid=(nb, n2c // FFT_COLS),
        in_specs=[pl.BlockSpec((2 * n1, n1), lambda b, t: (0, 0)), pl.BlockSpec((n1, tc), lambda b, t: (b, t))],
        out_specs=pl.BlockSpec((n1, tc), lambda b, t: (b, t)),
        out_shape=jax.ShapeDtypeStruct((nb * n1, n2c * w2), BF16),
        compiler_params=_params("arbitrary", "arbitrary"),
        name="fft_major",
    )(cs, flat)
    a = a.reshape(nb * L, w2)
    rows = grp * n2c
    ng = n1 // grp
    out = pl.pallas_call(
        _fft_minor_kernel,
        grid=(ng, nb),
        in_specs=[pl.BlockSpec((None, rows, rows), lambda g, b: (g, 0, 0)),
                  pl.BlockSpec((None, rows, rows), lambda g, b: (g, 0, 0)),
                  pl.BlockSpec((rows, w2), lambda g, b: (b * ng + g, 0))],
        out_specs=pl.BlockSpec((None, n2c, grp, FNET_COLS), lambda g, b: (b, 0, g, 0)),
        out_shape=jax.ShapeDtypeStruct((nb, n2c, n1, FNET_COLS), F32),
        compiler_params=_params("arbitrary", "arbitrary"),
        name="fft_minor",
    )(mc, ms, a)
    return out.reshape(nb * L, FNET_COLS)


def _merge_kernel(b0_ref, b1_ref, b2_ref, b3_ref, g_ref, wb_ref, wo_ref, o_ref):
    D = D_MODEL
    m = None
    for n, b_ref in enumerate((b0_ref, b1_ref, b2_ref, b3_ref)):
        proj = jnp.dot(b_ref[...].astype(BF16), wb_ref[n], preferred_element_type=F32)
        t = jax.nn.sigmoid(g_ref[:, n * D:(n + 1) * D].astype(F32)) * proj
        m = t if m is None else m + t
    o_ref[...] = jnp.dot(m.astype(BF16), wo_ref[...], preferred_element_type=F32)


def _merge(branches, gates, w_branch, w_out):
    R = gates.shape[0]
    D = D_MODEL
    tm = MERGE_TILE
    specs = [pl.BlockSpec((tm, BRANCH_W), lambda i: (i, 0))] * N_BRANCH
    specs += [pl.BlockSpec((tm, N_BRANCH * D), lambda i: (i, 0)),
              pl.BlockSpec((N_BRANCH, BRANCH_W, D), lambda i: (0, 0, 0)),
              pl.BlockSpec((D, D), lambda i: (0, 0))]
    return pl.pallas_call(
        _merge_kernel,
        grid=(R // tm,),
        in_specs=specs,
        out_specs=pl.BlockSpec((tm, D), lambda i: (i, 0)),
        out_shape=jax.ShapeDtypeStruct((R, D), F32),
        compiler_params=_params("arbitrary"),
        name="merge",
    )(*branches, gates, w_branch, w_out)


def _moe_plan(rec, counts, R):
    T = MOE_TILE
    cnt = counts[:, 0].astype(jnp.int32)
    padded = ((cnt + T - 1) // T) * T
    ends = jnp.cumsum(padded)
    base = ends - padded
    n_tiles = (2 * R) // T + N_EXPERTS
    n_used = (ends[-1] // T).astype(jnp.int32)
    tile = jnp.minimum(jnp.arange(n_tiles, dtype=jnp.int32), n_used - 1)
    tile_expert = jnp.sum((ends // T)[None, :] <= tile[:, None], axis=1).astype(jnp.int32)
    tile_expert = jnp.minimum(tile_expert, N_EXPERTS - 1)
    e_ab = rec[0:2].astype(jnp.int32)
    r_ab = rec[2:4].astype(jnp.int32)
    base_ab = jnp.zeros_like(e_ab)
    for e in range(N_EXPERTS):
        base_ab = jnp.where(e_ab == e, base[e], base_ab)
    pos = (base_ab + r_ab - 1).T.reshape(2 * R)
    wgt = rec[4:6].T
    tail = jnp.concatenate([ends - T, (cnt > 0).astype(jnp.int32), n_used.reshape(1)]).astype(jnp.int32)
    return pos, wgt, tile_expert, n_used.reshape(1), tail


def _row_copy(src, src_row, dst, dst_row, sem):
    return pltpu.make_async_copy(src.at[pl.ds(src_row, 1)], dst.at[pl.ds(dst_row, 1)], sem)


def _dispatch_kernel(pos_ref, tail_ref, h_ref, buf_ref, zero_ref, sem, zsem, *, tm, n_tiles):
    i = pl.program_id(0)

    @pl.when(i == 0)
    def _():
        zero_ref[...] = jnp.zeros_like(zero_ref)

        def tail_copy(e):
            row = pl.multiple_of(tail_ref[e], MOE_TILE)
            return pltpu.make_async_copy(zero_ref, buf_ref.at[pl.ds(row, MOE_TILE)], zsem)

        for e in range(N_EXPERTS):
            @pl.when(tail_ref[N_EXPERTS + e] > 0)
            def _():
                tail_copy(e).start()
        for e in range(N_EXPERTS):
            @pl.when(tail_ref[N_EXPERTS + e] > 0)
            def _():
                tail_copy(e).wait()

        def unused_copy(t):
            return pltpu.make_async_copy(zero_ref, buf_ref.at[pl.ds(pl.multiple_of(t * MOE_TILE, MOE_TILE), MOE_TILE)],
                                         zsem)

        def clear(t, carry):
            unused_copy(t).start()
            unused_copy(t).wait()
            return carry

        lax.fori_loop(tail_ref[2 * N_EXPERTS], n_tiles, clear, 0)

    def issue(t, carry):
        for slot in range(2):
            _row_copy(h_ref, t, buf_ref, pos_ref[(i * tm + t) * 2 + slot], sem).start()
        return carry

    lax.fori_loop(0, tm, issue, 0, unroll=8)
    for slot in range(2):
        pltpu.make_async_copy(h_ref, buf_ref.at[pl.ds(0, tm)], sem).wait()


def _moe_dispatch(h, pos, tail, n_rows):
    R, D = h.shape
    tm = ROW_TILE
    return pl.pallas_call(
        functools.partial(_dispatch_kernel, tm=tm, n_tiles=n_rows // MOE_TILE),
        grid_spec=pltpu.PrefetchScalarGridSpec(
            num_scalar_prefetch=2,
            grid=(R // tm,),
            in_specs=[pl.BlockSpec((tm, D), lambda i, pos, tail: (i, 0))],
            out_specs=pl.BlockSpec(memory_space=pl.ANY),
            scratch_shapes=[pltpu.VMEM((MOE_TILE, D), F32), pltpu.SemaphoreType.DMA, pltpu.SemaphoreType.DMA],
        ),
        out_shape=jax.ShapeDtypeStruct((n_rows, D), F32),
        compiler_params=_params("arbitrary"),
        name="moe_dispatch",
    )(pos, tail, h)


def _experts_kernel(te_ref, nu_ref, x_ref, w1_ref, w3_ref, w2_ref, o_ref):
    @pl.when(pl.program_id(0) >= nu_ref[0])
    def _():
        o_ref[...] = jnp.zeros_like(o_ref)

    @pl.when(pl.program_id(0) < nu_ref[0])
    def _():
        x = x_ref[...].astype(BF16)
        a = jnp.dot(x, w1_ref[...], preferred_element_type=F32)
        b = jnp.dot(x, w3_ref[...], preferred_element_type=F32)
        hid = (a * jax.nn.sigmoid(a) * b).astype(BF16)
        o_ref[...] = jnp.dot(hid, w2_ref[...], preferred_element_type=F32)


def _moe_experts(xs, tile_expert, n_used, w1, w3, w2):
    P, D = xs.shape
    T = MOE_TILE
    row = pl.BlockSpec((T, D), lambda i, te, nu: (i, 0))
    return pl.pallas_call(
        _experts_kernel,
        grid_spec=pltpu.PrefetchScalarGridSpec(
            num_scalar_prefetch=2,
            grid=(P // T,),
            in_specs=[row,
                      pl.BlockSpec((None, D, D_FF), lambda i, te, nu: (te[i], 0, 0)),
                      pl.BlockSpec((None, D, D_FF), lambda i, te, nu: (te[i], 0, 0)),
                      pl.BlockSpec((None, D_FF, D), lambda i, te, nu: (te[i], 0, 0))],
            out_specs=row,
        ),
        out_shape=jax.ShapeDtypeStruct((P, D), F32),
        compiler_params=_params("arbitrary"),
        name="moe_experts",
    )(tile_expert, n_used, xs, w1, w3, w2)


def _combine_norm_kernel(*refs, tc, n_tiles, alpha, emit_h):
    refs = list(refs)
    pos_ref, x_ref, w_ref, g_ref, lng_ref, lnb_ref = refs[:6]
    refs = refs[6:]
    if emit_h:
        sc_ref, sh_ref = refs[:2]
        refs = refs[2:]
    y_ref, xo_ref = refs[:2]
    refs = refs[2:]
    if emit_h:
        h_ref = refs.pop(0)
    ya_ref, yb_ref, sem = refs
    i = pl.program_id(0)

    def gather(tile, slot):
        def issue(t, carry):
            p = (tile * tc + t) * 2
            _row_copy(y_ref, pos_ref[p], ya_ref.at[slot], t, sem.at[slot]).start()
            _row_copy(y_ref, pos_ref[p + 1], yb_ref.at[slot], t, sem.at[slot]).start()
            return carry
        lax.fori_loop(0, tc, issue, 0, unroll=8)

    @pl.when(i == 0)
    def _():
        gather(0, 0)

    @pl.when(i + 1 < n_tiles)
    def _():
        gather(i + 1, (i + 1) % 2)

    slot = i % 2
    for buf in (ya_ref, yb_ref):
        pltpu.make_async_copy(y_ref.at[pl.ds(0, tc)], buf.at[slot], sem.at[slot]).wait()
    w = w_ref[...]
    f = w[:, 0:1] * ya_ref[slot] + w[:, 1:2] * yb_ref[slot]
    xn = _standardize(alpha * x_ref[...] + g_ref[...] * f) * lng_ref[...] + lnb_ref[...]
    xo_ref[...] = xn
    if emit_h:
        h_ref[...] = (xn * (1.0 + sc_ref[...]) + sh_ref[...]).astype(BF16)


def _moe_combine_norm(x, ys, pos, wgt, *, seg_of_row, alpha, mod_g, k_gate, ln_g, ln_b, mod_h=None, k_scale=None,
                      k_shift=None, name):
    R, D = x.shape
    tc = COMBINE_TILE
    emit_h = mod_h is not None
    row = pl.BlockSpec((tc, D), lambda i, pos: (i, 0))

    def mod_spec(k):
        return pl.BlockSpec((None, 1, D), lambda i, pos: (seg_of_row(i * tc) * 6 + k, 0, 0))

    vec = pl.BlockSpec((1, D), lambda i, pos: (0, 0))
    args = [x, wgt, mod_g, ln_g.reshape(1, D), ln_b.reshape(1, D)]
    specs = [row, pl.BlockSpec((tc, 2), lambda i, pos: (i, 0)), mod_spec(k_gate), vec, vec]
    if emit_h:
        args += [mod_h, mod_h]
        specs += [mod_spec(k_scale), mod_spec(k_shift)]
    args.append(ys)
    specs.append(pl.BlockSpec(memory_space=pl.ANY))
    out_shape = [jax.ShapeDtypeStruct((R, D), F32)]
    out_specs = [row]
    if emit_h:
        out_shape.append(jax.ShapeDtypeStruct((R, D), BF16))
        out_specs.append(row)
    return pl.pallas_call(
        functools.partial(_combine_norm_kernel, tc=tc, n_tiles=R // tc, alpha=alpha, emit_h=emit_h),
        grid_spec=pltpu.PrefetchScalarGridSpec(
            num_scalar_prefetch=1,
            grid=(R // tc,),
            in_specs=specs,
            out_specs=out_specs,
            scratch_shapes=[pltpu.VMEM((2, tc, D), F32), pltpu.VMEM((2, tc, D), F32), pltpu.SemaphoreType.DMA((2,))],
        ),
        out_shape=out_shape,
        compiler_params=_params("arbitrary"),
        name=name,
    )(pos, *args)


def _pad_heads(w, heads, width, lo, hi):
    lead = w.shape[:-1]
    w = w.reshape(lead + (heads, width))[..., lo:hi]
    w = jnp.pad(w, ((0, 0),) * (len(lead) + 1) + ((0, HEAD_PAD - (hi - lo)),))
    return w.reshape(lead + (heads * HEAD_PAD,))


def kernel(x, c, ctx, c_ctx, w_ada, b_ada, w_in, mla_q_norm, mla_w_uq, mla_kv_norm, mla_w_ukv, diff_lq1, diff_lk1,
           diff_lq2, diff_lk2, diff_subln, conv_w, conv_b, conv_ln_g, conv_ln_b, w_branch, w_out, ln1_g, ln1_b,
           ln2_g, ln2_b, router_w, router_bias, exp_w1, exp_w3, exp_w2):
    B, S, D = x.shape
    C = ctx.shape[1]
    depth = w_in.shape[0]
    N, NC = B * S, B * C
    tm = ROW_TILE
    assert D == D_MODEL and S % tm == 0 and NC % tm == 0 and S % C == 0 and S % GRID_W == 0

    R = N + NC

    def seg_of_row(r):
        return jnp.minimum(r // S, B)

    def seg(i):
        return seg_of_row(i * tm)

    def pos_block(i):
        return jnp.where(i < N // tm, i % (S // tm), S // tm)

    rows = S // GRID_W
    tabs_mq = _rope_tables(rows, MLA_DR, MLA_DN, HEAD_PAD, MLA_HEADS, tm)
    tabs_mk = _rope_tables(rows, MLA_DR, 0, LANES, 1, tm)
    tabs_d = _rope_tables(rows, DIFF_D, 0, DIFF_D, 2 * DIFF_HEADS, tm)

    nseg = 8
    cond = jnp.zeros((nseg, D), F32).at[:B].set(c).at[B].set(c_ctx)
    mod_all = _ada(cond, w_ada, b_ada).reshape(depth, nseg * 6, 1, D)

    dft_ch = _dft_channel_matrix().astype(BF16)
    dft_ctx = _dft_position_matrix(C).astype(BF16)
    if S % (FFT_MINOR * FFT_GROUP) == 0:
        fnet_lat = lambda y: _fnet_positions_fft(y, B, S)
    else:
        dft_lat = _dft_position_matrix(S).astype(BF16)
        fnet_lat = lambda y: _fnet_positions(y, dft_lat, 0, B, S)
    rw_t = router_w.T
    rw_hi = rw_t.astype(BF16)
    rw_lo = (rw_t - rw_hi.astype(F32)).astype(BF16)
    router = (rw_hi, rw_lo, router_bias.reshape(N_EXPERTS, 1))

    xs = jnp.concatenate([x.reshape(N, D), ctx.reshape(NC, D)], axis=0)
    alpha = (2.0 * depth) ** 0.25
    xs, h = _norm(xs, seg=seg, alpha=alpha, first=True, mod_h=mod_all[0], k_scale=1, k_shift=0, name="norm_in")

    w_mla = jnp.pad(w_in[:, :, :OFF_DIFF], ((0, 0), (0, 0), (0, MLA_COLS_PAD - MLA_COLS))).astype(BF16)
    w_diff = w_in[:, :, OFF_DIFF:OFF_CONV].astype(BF16)
    w_conv = w_in[:, :, OFF_CONV:OFF_FNET].astype(BF16)
    w_fnet = w_in[:, :, OFF_FNET:OFF_GATE].astype(BF16)
    w_gate = w_in[:, :, OFF_GATE:].astype(BF16)
    wq = _pad_heads(mla_w_uq, MLA_HEADS, MLA_DN + MLA_DR, 0, MLA_DN + MLA_DR).astype(BF16)
    wk = _pad_heads(mla_w_ukv, MLA_HEADS, MLA_DN + MLA_DV, 0, MLA_DN).astype(BF16)
    wv = _pad_heads(mla_w_ukv, MLA_HEADS, MLA_DN + MLA_DV, MLA_DN, MLA_DN + MLA_DV).astype(BF16)

    for l in range(depth):
        mod = mod_all[l]
        cm = _project(h, w_mla, F32, "proj_mla", MLA_COLS_PAD, layer=l)
        cd = _project(h, w_diff, BF16, "proj_diff", DIFF_COLS, layer=l)
        cc = _project(h, w_conv, BF16, "proj_conv", CONV_COLS, layer=l)
        cf = _project(h, w_fnet, BF16, "proj_fnet", FNET_COLS, layer=l)
        gates = _project(h, w_gate, BF16, "proj_gate", 2048, layer=l)

        mq, mk, mv = _mla_prep(cm, mla_q_norm[l].reshape(1, -1), mla_kv_norm[l].reshape(1, -1), wq[l], wk[l], wv[l],
                               tabs_mq, tabs_mk, pos_block)
        o_mla = jnp.concatenate([_attention("mla", mq, mk, mv, B=B, S=S, C=C, ctx_queries=False),
                                 _attention("mla", mq, mk, mv, B=B, S=S, C=C, ctx_queries=True)], axis=0)

        lam_init = 0.8 - 0.6 * math.exp(-0.3 * l)
        ll = jnp.stack([diff_lq1[l], diff_lk1[l], diff_lq2[l], diff_lk2[l]], axis=0)
        sub = diff_subln[l].reshape(1, -1)
        dq, dk, dv = _diff_prep(cd, tabs_d, pos_block)
        o_diff = jnp.concatenate(
            [_attention("diff", dq, dk, dv, B=B, S=S, C=C, ctx_queries=False, extra=(ll, sub), lam_init=lam_init),
             _attention("diff", dq, dk, dv, B=B, S=S, C=C, ctx_queries=True, extra=(ll, sub), lam_init=lam_init)],
            axis=0)

        o_conv = jnp.concatenate([_conv(cc, 0, B, S, conv_w[l], conv_b[l], conv_ln_g[l], conv_ln_b[l]),
                                  _conv(cc, N, B, C, conv_w[l], conv_b[l], conv_ln_g[l], conv_ln_b[l])], axis=0)
        y = _project(cf, dft_ch, BF16, "fnet_ch", 2 * FNET_COLS)
        o_fnet = jnp.concatenate([fnet_lat(y), _fnet_positions(y, dft_ctx, N, B, C)], axis=0)

        mix = _merge([o_mla, o_diff, o_conv, o_fnet], gates, w_branch[l].astype(BF16), w_out[l].astype(BF16))
        xs, h2, rec, counts = _norm(xs, seg=seg, alpha=alpha, m=mix, mod_g=mod, k_gate=2, ln_g=ln1_g[l],
                                    ln_b=ln1_b[l], mod_h=mod, k_scale=4, k_shift=3, router=router, name="norm1")
        pos, wgt, tile_expert, n_used, tail = _moe_plan(rec, counts, R)
        sorted_rows = _moe_dispatch(h2, pos, tail, tile_expert.shape[0] * MOE_TILE)
        ys = _moe_experts(sorted_rows, tile_expert, n_used, exp_w1[l].astype(BF16), exp_w3[l].astype(BF16),
                          exp_w2[l].astype(BF16))
        last = l + 1 == depth
        out = _moe_combine_norm(xs, ys, pos, wgt, seg_of_row=seg_of_row, alpha=alpha, mod_g=mod, k_gate=5,
                                ln_g=ln2_g[l], ln_b=ln2_b[l], mod_h=None if last else mod_all[l + 1],
                                k_scale=1, k_shift=0, name="norm_out" if last else "norm2")
        if last:
            (xs,) = out
        else:
            xs, h = out
    return xs[:N].reshape(B, S, D)
```

```python
import functools
import math

import jax
import jax.numpy as jnp
from jax import lax
from jax.experimental import pallas as pl
from jax.experimental.pallas import tpu as pltpu

F32 = jnp.float32
BF16 = jnp.bfloat16

D_MODEL = 2048
GRID_W = 64
ROPE_THETA = 10000.0
EPS = 1e-6

MLA_HEADS = 8
MLA_DN = 64
MLA_DR = 32
MLA_DV = 64
MLA_Q_RANK = 384
MLA_KV_RANK = 256
DIFF_HEADS = 4
DIFF_D = 64
CONV_W = 512
CONV_K = 31
FNET_GROUPS = 4
FNET_GROUP_W = 128
N_BRANCH = 4
BRANCH_W = 512
N_EXPERTS = 16
N_GROUPS = 4
EXPERTS_PER_GROUP = N_EXPERTS // N_GROUPS
D_FF = 1024

MLA_COLS = MLA_Q_RANK + MLA_KV_RANK + MLA_DR
DIFF_COLS = 3 * DIFF_HEADS * 2 * DIFF_D
CONV_COLS = 2 * CONV_W
FNET_COLS = FNET_GROUPS * FNET_GROUP_W
GATE_COLS = N_BRANCH * D_MODEL
OFF_DIFF = MLA_COLS
OFF_CONV = OFF_DIFF + DIFF_COLS
OFF_FNET = OFF_CONV + CONV_COLS
OFF_GATE = OFF_FNET + FNET_COLS

LANES = 128
SUBLANES = 8
HEAD_PAD = 128
MLA_COLS_PAD = 768
CONV_HALO = 16
VMEM_LIMIT = 56 * 1024 * 1024
LOG2E = math.log2(math.e)

ROW_TILE = 512
MERGE_TILE = 256
MOE_TILE = 512
COMBINE_TILE = 256
ATTN_TQ = 512
ATTN_TK = {"mla": 2048, "diff": 1024}


def _params(*sem):
    return pltpu.CompilerParams(dimension_semantics=("arbitrary",) * len(sem), vmem_limit_bytes=VMEM_LIMIT)


def _mm_kernel(a_ref, w_ref, o_ref, acc_ref, *, nk):
    prod = jnp.dot(a_ref[...], w_ref[...], preferred_element_type=F32)
    if nk == 1:
        o_ref[...] = prod.astype(o_ref.dtype)
        return
    k = pl.program_id(3)

    @pl.when(k == 0)
    def _():
        acc_ref[...] = prod

    @pl.when(k > 0)
    def _():
        acc_ref[...] += prod

    @pl.when(k == nk - 1)
    def _():
        o_ref[...] = acc_ref[...].astype(o_ref.dtype)


def _matmul(a, w, *, out_shape, out_dtype, tm, tn, tk, grid, a_map, o_map, name, w_map=None, w_spec=None):
    nk = grid[3]
    if w_spec is None:
        w_spec = pl.BlockSpec((tk, tn), w_map)
    return pl.pallas_call(
        functools.partial(_mm_kernel, nk=nk),
        grid=grid,
        in_specs=[pl.BlockSpec((tm, tk), a_map), w_spec],
        out_specs=pl.BlockSpec((tm, tn), o_map),
        out_shape=jax.ShapeDtypeStruct(out_shape, out_dtype),
        scratch_shapes=[pltpu.VMEM((tm, tn), F32)],
        compiler_params=_params("parallel", "parallel", "parallel", "arbitrary"),
        name=name,
    )(a, w)


def _project(h, w, out_dtype, name, tn, layer=None):
    R, K = h.shape
    N = w.shape[-1]
    tm = ROW_TILE
    if layer is None:
        w_spec = pl.BlockSpec((K, tn), lambda i, j, u, k: (0, j))
    else:
        w_spec = pl.BlockSpec((None, K, tn), lambda i, j, u, k: (layer, 0, j))
    return _matmul(h, w, out_shape=(R, N), out_dtype=out_dtype, tm=tm, tn=tn, tk=K,
                   grid=(R // tm, N // tn, 1, 1),
                   a_map=lambda i, j, u, k: (i, 0), w_spec=w_spec,
                   o_map=lambda i, j, u, k: (i, j), name=name)


def _ada_kernel(c_ref, w_ref, b_ref, o_ref):
    c = c_ref[...]
    a = (c * jax.nn.sigmoid(c)).astype(BF16)
    o_ref[...] = jnp.dot(a, w_ref[...].astype(BF16), preferred_element_type=F32) + b_ref[...]


def _ada(cond, w_ada, b_ada):
    L, D, N6 = w_ada.shape
    rows = cond.shape[0]
    tn = 1024
    return pl.pallas_call(
        _ada_kernel,
        grid=(L, N6 // tn),
        in_specs=[pl.BlockSpec((rows, D), lambda l, j: (0, 0)),
                  pl.BlockSpec((None, D, tn), lambda l, j: (l, 0, j)),
                  pl.BlockSpec((None, 1, tn), lambda l, j: (l, 0, j))],
        out_specs=pl.BlockSpec((None, rows, tn), lambda l, j: (l, 0, j)),
        out_shape=jax.ShapeDtypeStruct((L, rows, N6), F32),
        compiler_params=_params("parallel", "parallel"),
        name="ada_mod",
    )(cond, w_ada, b_ada.reshape(L, 1, N6))


def _route(logits_t, bias_t):
    scores = jax.nn.sigmoid(logits_t)
    sel = scores + bias_t
    sel_r = [sel[e:e + 1, :] for e in range(N_EXPERTS)]
    sc_r = [scores[e:e + 1, :] for e in range(N_EXPERTS)]
    per = EXPERTS_PER_GROUP
    gscore = []
    for g in range(N_GROUPS):
        r = sel_r[g * per:(g + 1) * per]
        best = None
        for a in range(per):
            for b in range(a + 1, per):
                s = r[a] + r[b]
                best = s if best is None else jnp.maximum(best, s)
        gscore.append(best)
    best_g = jnp.zeros_like(gscore[0], dtype=jnp.int32)
    best_v = gscore[0]
    for g in range(1, N_GROUPS):
        better = gscore[g] > best_v
        best_g = jnp.where(better, g, best_g)
        best_v = jnp.where(better, gscore[g], best_v)
    v = []
    s = []
    for j in range(per):
        vj = sel_r[j]
        sj = sc_r[j]
        for g in range(1, N_GROUPS):
            vj = jnp.where(best_g == g, sel_r[g * per + j], vj)
            sj = jnp.where(best_g == g, sc_r[g * per + j], sj)
        v.append(vj)
        s.append(sj)
    chosen = []
    for j in range(per):
        rank = jnp.zeros_like(best_g)
        for i in range(per):
            if i == j:
                continue
            ahead = (v[i] > v[j]) | ((v[i] == v[j]) & (i < j))
            rank = rank + ahead.astype(jnp.int32)
        chosen.append(rank < 2)
    total = jnp.zeros_like(s[0])
    for j in range(per):
        total = total + jnp.where(chosen[j], s[j], 0.0)
    picked, weight = [], []
    for e in range(N_EXPERTS):
        g, j = divmod(e, per)
        on = (best_g == g) & chosen[j]
        picked.append(on)
        weight.append(jnp.where(on, s[j] / total, 0.0))
    return picked, weight


def _dispatch_info(picked, weight, tri_ref, tot_ref):
    sel = jnp.concatenate([p.astype(F32) for p in picked], axis=0)
    cum = jnp.dot(sel.astype(BF16), tri_ref[...], preferred_element_type=F32)
    tot = tot_ref[...]
    rank = cum + tot[:, 0:1]
    tot_ref[...] = tot + jnp.sum(sel, axis=1, keepdims=True)
    zero = jnp.zeros_like(weight[0])
    seen = zero
    rec = [zero] * 6
    for e in range(N_EXPERTS):
        on = picked[e]
        for slot, is_slot in ((0, on & (seen == 0.0)), (1, on & (seen == 1.0))):
            rec[slot] = jnp.where(is_slot, float(e), rec[slot])
            rec[2 + slot] = jnp.where(is_slot, rank[e:e + 1, :], rec[2 + slot])
            rec[4 + slot] = jnp.where(is_slot, weight[e], rec[4 + slot])
        seen = seen + on.astype(F32)
    return jnp.concatenate(rec + [zero, zero], axis=0)


def _standardize(y):
    mu = jnp.mean(y, axis=-1, keepdims=True)
    d = y - mu
    var = jnp.mean(d * d, axis=-1, keepdims=True)
    return d * lax.rsqrt(var + EPS)


def _norm_kernel(*refs, first, emit_h, router, alpha):
    refs = list(refs)
    x_ref = refs.pop(0)
    if not first:
        m_ref, g_ref, lng_ref, lnb_ref = refs[:4]
        refs = refs[4:]
    if emit_h:
        sc_ref, sh_ref = refs[:2]
        refs = refs[2:]
    if router:
        rwh_ref, rwl_ref, rb_ref, tri_ref = refs[:4]
        refs = refs[4:]
    xo_ref = refs.pop(0)
    x = x_ref[...]
    xn = _standardize(x if first else alpha * x + g_ref[...] * m_ref[...])
    if not first:
        xn = xn * lng_ref[...] + lnb_ref[...]
    xo_ref[...] = xn
    if not emit_h:
        return
    h_ref = refs.pop(0)
    h = xn * (1.0 + sc_ref[...]) + sh_ref[...]
    h_ref[...] = h.astype(h_ref.dtype)
    if router:
        rec_ref, cnt_ref, tot_ref = refs

        @pl.when(pl.program_id(0) == 0)
        def _():
            tot_ref[...] = jnp.zeros_like(tot_ref)

        h_hi = h.astype(BF16)
        h_lo = (h - h_hi.astype(F32)).astype(BF16)
        nt = (((1,), (1,)), ((), ()))
        logits_t = (lax.dot_general(rwh_ref[...], h_hi, nt, preferred_element_type=F32)
                    + lax.dot_general(rwh_ref[...], h_lo, nt, preferred_element_type=F32)
                    + lax.dot_general(rwl_ref[...], h_hi, nt, preferred_element_type=F32))
        picked, weight = _route(logits_t, rb_ref[...])
        rec_ref[...] = _dispatch_info(picked, weight, tri_ref, tot_ref)
        cnt_ref[...] = tot_ref[...]


def _norm(x, *, seg, alpha, first=False, m=None, mod_g=None, k_gate=None, ln_g=None, ln_b=None,
          mod_h=None, k_scale=None, k_shift=None, router=None, name):
    R, D = x.shape
    tm = ROW_TILE
    emit_h = mod_h is not None
    row = pl.BlockSpec((tm, D), lambda i: (i, 0))

    def mod_spec(k):
        return pl.BlockSpec((None, 1, D), lambda i: (seg(i) * 6 + k, 0, 0))

    vec = pl.BlockSpec((1, D), lambda i: (0, 0))
    args, specs = [x], [row]
    if not first:
        args += [m, mod_g, ln_g.reshape(1, D), ln_b.reshape(1, D)]
        specs += [row, mod_spec(k_gate), vec, vec]
    if emit_h:
        args += [mod_h, mod_h]
        specs += [mod_spec(k_scale), mod_spec(k_shift)]
    scratch = []
    if router is not None:
        rwh, rwl, rb = router
        tri = (jnp.arange(tm)[:, None] <= jnp.arange(tm)[None, :]).astype(BF16)
        args += [rwh, rwl, rb, tri]
        specs += [pl.BlockSpec((N_EXPERTS, D), lambda i: (0, 0))] * 2
        specs += [pl.BlockSpec((N_EXPERTS, 1), lambda i: (0, 0)), pl.BlockSpec((tm, tm), lambda i: (0, 0))]
        scratch = [pltpu.VMEM((N_EXPERTS, LANES), F32)]
    out_shape = [jax.ShapeDtypeStruct((R, D), F32)]
    out_specs = [row]
    if emit_h:
        out_shape.append(jax.ShapeDtypeStruct((R, D), BF16 if router is None else F32))
        out_specs.append(row)
    if router is not None:
        out_shape += [jax.ShapeDtypeStruct((8, R), F32), jax.ShapeDtypeStruct((N_EXPERTS, LANES), F32)]
        out_specs += [pl.BlockSpec((8, tm), lambda i: (0, i)), pl.BlockSpec((N_EXPERTS, LANES), lambda i: (0, 0))]
    return pl.pallas_call(
        functools.partial(_norm_kernel, first=first, emit_h=emit_h, router=router is not None, alpha=alpha),
        grid=(R // tm,),
        in_specs=specs,
        out_specs=out_specs,
        out_shape=out_shape,
        scratch_shapes=scratch,
        compiler_params=_params("arbitrary" if router is not None else "parallel"),
        name=name,
    )(*args)


def _rope_tables(rows, rot_dim, lane0, chunk, reps, ident_rows):
    n = rows * GRID_W
    r = jnp.broadcast_to(jnp.arange(rows, dtype=F32)[:, None], (rows, GRID_W)).reshape(n)
    c = jnp.broadcast_to(jnp.arange(GRID_W, dtype=F32)[None, :], (rows, GRID_W)).reshape(n)
    quarter = rot_dim // 4
    inv_freq = ROPE_THETA ** (-jnp.arange(quarter, dtype=F32) / quarter)
    ar = r[:, None] * inv_freq
    ac = c[:, None] * inv_freq
    ang = jnp.concatenate([ar, ar, ac, ac], -1)
    sign = jnp.concatenate([-jnp.ones((quarter,), F32), jnp.ones((quarter,), F32)] * 2)
    cos = jnp.ones((n, chunk), F32).at[:, lane0:lane0 + rot_dim].set(jnp.cos(ang))
    sin = jnp.zeros((n, chunk), F32).at[:, lane0:lane0 + rot_dim].set(jnp.sin(ang) * sign)
    cos = jnp.concatenate([cos, jnp.ones((ident_rows, chunk), F32)], 0)
    sin = jnp.concatenate([sin, jnp.zeros((ident_rows, chunk), F32)], 0)
    return jnp.tile(cos, (1, reps)), jnp.tile(sin, (1, reps))


def _rope(x, cos, sin_signed, quarter, lane0):
    w = x.shape[-1]
    lane = lax.broadcasted_iota(jnp.int32, x.shape, 1)
    even = (((lane - lane0) // quarter) % 2) == 0
    partner = jnp.where(even, pltpu.roll(x, w - quarter, 1), pltpu.roll(x, quarter, 1))
    return x * cos + partner * sin_signed


def _rms(x, g):
    return x * lax.rsqrt(jnp.mean(x * x, axis=-1, keepdims=True) + EPS) * g


def _mla_prep_kernel(cm_ref, qn_ref, kvn_ref, wq_ref, wk_ref, wv_ref, cq_ref, sq_ref, ck_ref, sk_ref,
                     q_ref, k_ref, v_ref):
    cm = cm_ref[...]
    cq = cm[:, :MLA_Q_RANK]
    ckv = cm[:, MLA_Q_RANK:MLA_Q_RANK + MLA_KV_RANK]
    kr = cm[:, MLA_Q_RANK + MLA_KV_RANK:]
    qn = _rms(cq, qn_ref[...]).astype(BF16)
    kvn = _rms(ckv, kvn_ref[...]).astype(BF16)
    q = jnp.dot(qn, wq_ref[...], preferred_element_type=F32)
    q = _rope(q, cq_ref[...], sq_ref[...], MLA_DR // 4, MLA_DN)
    scale = (MLA_DN + MLA_DR) ** -0.5 * LOG2E
    q_ref[...] = (q * scale).astype(BF16)
    kr = _rope(kr, ck_ref[...], sk_ref[...], MLA_DR // 4, 0)
    kr = pltpu.roll(kr, MLA_DN, 1)
    k = jnp.dot(kvn, wk_ref[...], preferred_element_type=F32)
    k_ref[...] = (k + jnp.concatenate([kr] * MLA_HEADS, axis=1)).astype(BF16)
    v = jnp.dot(kvn, wv_ref[...], preferred_element_type=F32)
    lane = lax.broadcasted_iota(jnp.int32, v.shape, 1)
    v_ref[...] = jnp.where(lane % HEAD_PAD == MLA_DV, 1.0, v).astype(BF16)


def _mla_prep(cm, q_norm, kv_norm, wq, wk, wv, tabs_q, tabs_k, pos_block):
    R = cm.shape[0]
    tm = ROW_TILE
    W = MLA_HEADS * HEAD_PAD
    row = lambda w: pl.BlockSpec((tm, w), lambda i: (i, 0))
    full = lambda a: pl.BlockSpec(a.shape, lambda i: (0, 0))
    tab = lambda w: pl.BlockSpec((tm, w), lambda i: (pos_block(i), 0))
    outs = pl.pallas_call(
        _mla_prep_kernel,
        grid=(R // tm,),
        in_specs=[row(MLA_COLS_PAD), full(q_norm), full(kv_norm), full(wq), full(wk), full(wv),
                  tab(W), tab(W), tab(LANES), tab(LANES)],
        out_specs=[row(W)] * 3,
        out_shape=[jax.ShapeDtypeStruct((R, W), BF16)] * 3,
        compiler_params=_params("parallel"),
        name="mla_prep",
    )(cm, q_norm, kv_norm, wq, wk, wv, tabs_q[0], tabs_q[1], tabs_k[0], tabs_k[1])
    return outs


def _diff_prep_kernel(cd_ref, cos_ref, sin_ref, q_ref, k_ref, v_ref):
    w = DIFF_HEADS * 2 * DIFF_D
    cd = cd_ref[...].astype(F32)
    cos = cos_ref[...]
    sin = sin_ref[...]
    q = _rope(cd[:, :w], cos, sin, DIFF_D // 4, 0)
    q_ref[...] = (q * (DIFF_D ** -0.5 * LOG2E)).astype(BF16)
    k_ref[...] = _rope(cd[:, w:2 * w], cos, sin, DIFF_D // 4, 0).astype(BF16)
    v = cd_ref[:, 2 * w:]
    lane = lax.broadcasted_iota(jnp.int32, (v.shape[0], LANES), 1)
    ones = jnp.where(lane == 0, 1.0, 0.0).astype(BF16)
    parts = []
    for h in range(DIFF_HEADS):
        parts += [v[:, h * 2 * DIFF_D:(h + 1) * 2 * DIFF_D], ones]
    v_ref[...] = jnp.concatenate(parts, axis=1)


def _diff_prep(cd, tabs, pos_block):
    R = cd.shape[0]
    tm = ROW_TILE
    w = DIFF_HEADS * 2 * DIFF_D
    row = lambda c: pl.BlockSpec((tm, c), lambda i: (i, 0))
    tab = pl.BlockSpec((tm, w), lambda i: (pos_block(i), 0))
    return pl.pallas_call(
        _diff_prep_kernel,
        grid=(R // tm,),
        in_specs=[row(DIFF_COLS), tab, tab],
        out_specs=[row(w), row(w), row(2 * w)],
        out_shape=[jax.ShapeDtypeStruct((R, w), BF16), jax.ShapeDtypeStruct((R, w), BF16),
                   jax.ShapeDtypeStruct((R, 2 * w), BF16)],
        compiler_params=_params("parallel"),
        name="diff_prep",
    )(cd, tabs[0], tabs[1])


def _flash(streams, sources, vw, unroll):
    tq = streams[0][0].shape[0]
    carry = tuple((jnp.full((tq, 1), -jnp.inf, F32), jnp.zeros((tq, vw), F32)) for _ in streams)
    for k_ref, v_ref, length, tk in sources:
        def step(c, carry, k_ref=k_ref, v_ref=v_ref, tk=tk):
            r0 = pl.multiple_of(c * tk, tk)
            out = []
            for (q, kl0, vl0), (m, acc) in zip(streams, carry):
                k = k_ref[pl.ds(r0, tk), kl0:kl0 + HEAD_PAD]
                v = v_ref[pl.ds(r0, tk), vl0:vl0 + vw]
                s = lax.dot_general(q, k, (((1,), (1,)), ((), ())), preferred_element_type=F32)
                m_new = jnp.maximum(m, jnp.max(s, axis=1, keepdims=True))
                p = jnp.exp2(s - m_new)
                a = jnp.exp2(m - m_new)
                acc = a * acc + jnp.dot(p.astype(BF16), v, preferred_element_type=F32)
                out.append((m_new, acc))
            return tuple(out)
        n = length // tk
        if n == 1:
            carry = step(0, carry)
        else:
            carry = lax.fori_loop(0, n, step, carry, unroll=unroll)
    return [acc for _, acc in carry]


def _mla_attn_kernel(*refs, n_src, lens, tk, unroll):
    q_ref = refs[0]
    kv = refs[1:1 + 2 * n_src]
    o_ref = refs[1 + 2 * n_src]
    streams = [(q_ref[:, hh * HEAD_PAD:(hh + 1) * HEAD_PAD], hh * HEAD_PAD, hh * HEAD_PAD) for hh in range(2)]
    sources = [(kv[2 * s], kv[2 * s + 1], lens[s], min(tk, lens[s])) for s in range(n_src)]
    accs = _flash(streams, sources, HEAD_PAD, unroll)
    outs = [acc[:, :MLA_DV] / acc[:, MLA_DV:MLA_DV + 1] for acc in accs]
    o_ref[...] = jnp.concatenate(outs, axis=1).astype(BF16)


def _diff_attn_kernel(*refs, n_src, lens, tk, unroll, lam_init):
    q_ref, ll_ref, sub_ref = refs[:3]
    kv = refs[3:3 + 2 * n_src]
    o_ref = refs[3 + 2 * n_src]
    ll = ll_ref[...]
    lam = (jnp.exp(jnp.sum(ll[0:1, :] * ll[1:2, :], axis=1, keepdims=True))
           - jnp.exp(jnp.sum(ll[2:3, :] * ll[3:4, :], axis=1, keepdims=True)) + lam_init)
    q = q_ref[...]
    lane = lax.broadcasted_iota(jnp.int32, q.shape, 1)
    vw = 2 * HEAD_PAD
    dv = 2 * DIFF_D
    streams = [(jnp.where((lane // DIFF_D) == mp, q, jnp.zeros_like(q)), 0, 0) for mp in range(2)]
    sources = [(kv[2 * s], kv[2 * s + 1], lens[s], min(tk, lens[s])) for s in range(n_src)]
    accs = _flash(streams, sources, vw, unroll)
    o = [acc[:, :dv] / acc[:, dv:dv + 1] for acc in accs]
    od = o[0] - lam * o[1]
    o_ref[...] = (_rms(od, sub_ref[...]) * (1.0 - lam_init)).astype(BF16)


def _attention(kind, q, k, v, *, B, S, C, ctx_queries, extra=None, lam_init=None):
    N = B * S
    tk = ATTN_TK[kind]
    unroll = max(S // tk, 1)
    if kind == "mla":
        steps, qw, kw, vw, ow = MLA_HEADS // 2, 2 * HEAD_PAD, 2 * HEAD_PAD, 2 * HEAD_PAD, 2 * MLA_DV
    else:
        steps, qw, kw, vw, ow = DIFF_HEADS, HEAD_PAD, HEAD_PAD, 2 * HEAD_PAD, 2 * DIFF_D
    cblk = N // C
    if ctx_queries:
        tq, nq, rows_out = C, 1, B * C
        q_spec = pl.BlockSpec((tq, qw), lambda b, h, i: (cblk + b, h))
        kv_specs = [pl.BlockSpec((C, kw), lambda b, h, i: (cblk + b, h)),
                    pl.BlockSpec((C, vw), lambda b, h, i: (cblk + b, h))]
        lens = (C,)
    else:
        tq = ATTN_TQ
        nq, rows_out = S // tq, N
        q_spec = pl.BlockSpec((tq, qw), lambda b, h, i: (b * nq + i, h))
        kv_specs = [pl.BlockSpec((S, kw), lambda b, h, i: (b, h)),
                    pl.BlockSpec((S, vw), lambda b, h, i: (b, h)),
                    pl.BlockSpec((C, kw), lambda b, h, i: (cblk + b, h)),
                    pl.BlockSpec((C, vw), lambda b, h, i: (cblk + b, h))]
        lens = (S, C)
    n_src = len(lens)
    kv_args = [k, v] * n_src
    o_spec = pl.BlockSpec((tq, ow), lambda b, h, i: (b * nq + i, h))
    if kind == "mla":
        body = functools.partial(_mla_attn_kernel, n_src=n_src, lens=lens, tk=tk, unroll=unroll)
        args, specs = [q], [q_spec]
    else:
        ll, sub = extra
        body = functools.partial(_diff_attn_kernel, n_src=n_src, lens=lens, tk=tk, unroll=unroll, lam_init=lam_init)
        args = [q, ll, sub]
        specs = [q_spec, pl.BlockSpec(ll.shape, lambda b, h, i: (0, 0)), pl.BlockSpec(sub.shape, lambda b, h, i: (0, 0))]
    return pl.pallas_call(
        body,
        grid=(B, steps, nq),
        in_specs=specs + kv_specs,
        out_specs=o_spec,
        out_shape=jax.ShapeDtypeStruct((rows_out, steps * ow), BF16),
        compiler_params=_params("parallel", "parallel", "parallel"),
        name=f"{kind}_attn_{'ctx' if ctx_queries else 'lat'}",
    )(*args, *kv_args)


def _conv_kernel(main_ref, prev_ref, next_ref, w_ref, b_ref, g_ref, bb_ref, o_ref, z_ref, sh_ref, *, ts, nt):
    i = pl.program_id(1)

    def glu(blk):
        a = blk[:, :CONV_W].astype(F32)
        g = blk[:, CONV_W:].astype(F32)
        return a * jax.nn.sigmoid(g)

    H = CONV_HALO
    z_ref[0:H, :] = glu(prev_ref[...]) * (i > 0).astype(F32)
    z_ref[H:H + ts, :] = glu(main_ref[...])
    z_ref[H + ts:2 * H + ts, :] = glu(next_ref[...]) * (i < nt - 1).astype(F32)
    span = sh_ref.shape[1]
    for s in range(1, SUBLANES):
        sh_ref[s] = z_ref[s:s + span, :]
    rc = 32
    first = H - CONV_K // 2
    for r in range(ts // rc):
        acc = jnp.broadcast_to(b_ref[...], (rc, CONV_W))
        for k in range(CONV_K):
            s, a = (k + first) % SUBLANES, (k + first) // SUBLANES * SUBLANES
            src = z_ref[r * rc + a:r * rc + a + rc, :] if s == 0 else sh_ref[s, r * rc + a:r * rc + a + rc, :]
            acc = acc + w_ref[k:k + 1, :] * src
        mu = jnp.mean(acc, axis=-1, keepdims=True)
        d = acc - mu
        var = jnp.mean(d * d, axis=-1, keepdims=True)
        y = d * lax.rsqrt(var + EPS) * g_ref[...] + bb_ref[...]
        o_ref[r * rc:(r + 1) * rc, :] = (y * jax.nn.sigmoid(y)).astype(BF16)


def _conv(cols, row_off, nb, L, conv_w, conv_b, ln_g, ln_b):
    R = cols.shape[0]
    ts = min(ROW_TILE, L)
    nt = L // ts
    H = CONV_HALO
    hb = ts // H
    base = lambda b, i: (row_off + b * L + i * ts) // H
    vec = lambda a: pl.BlockSpec(a.shape, lambda b, i: (0, 0))
    cb, g, bb = conv_b.reshape(1, CONV_W), ln_g.reshape(1, CONV_W), ln_b.reshape(1, CONV_W)
    return pl.pallas_call(
        functools.partial(_conv_kernel, ts=ts, nt=nt),
        grid=(nb, nt),
        in_specs=[pl.BlockSpec((ts, CONV_COLS), lambda b, i: ((row_off + b * L) // ts + i, 0)),
                  pl.BlockSpec((H, CONV_COLS), lambda b, i: (jnp.maximum(base(b, i) - 1, 0), 0)),
                  pl.BlockSpec((H, CONV_COLS), lambda b, i: (jnp.minimum(base(b, i) + hb, R // H - 1), 0)),
                  vec(conv_w), vec(cb), vec(g), vec(bb)],
        out_specs=pl.BlockSpec((ts, CONV_W), lambda b, i: (b * nt + i, 0)),
        out_shape=jax.ShapeDtypeStruct((nb * L, CONV_W), BF16),
        scratch_shapes=[pltpu.VMEM((ts + 2 * H, CONV_W), F32),
                        pltpu.VMEM((SUBLANES, ts + 2 * H - SUBLANES, CONV_W), F32)],
        compiler_params=_params("parallel", "parallel"),
        name=f"conv_{L}",
    )(cols, cols, cols, conv_w, cb, g, bb)


def _dft_channel_matrix():
    j = jnp.arange(FNET_GROUP_W, dtype=jnp.int32)
    ang = (2.0 * math.pi / FNET_GROUP_W) * ((j[:, None] * j[None, :]) % FNET_GROUP_W).astype(F32)
    eye = jnp.eye(FNET_GROUPS, dtype=F32)
    scale = FNET_GROUP_W ** -0.5
    return jnp.concatenate([jnp.kron(eye, jnp.cos(ang)), jnp.kron(eye, jnp.sin(ang))], axis=1) * scale


def _dft_position_matrix(L):
    n = jnp.arange(L, dtype=jnp.int32)
    ang = (2.0 * math.pi / L) * ((n[:, None] * n[None, :]) % L).astype(F32)
    return jnp.concatenate([jnp.cos(ang), -jnp.sin(ang)], axis=1) * (L ** -0.5)


def _fnet_positions(y, a, row_off, nb, L):
    tm = min(ROW_TILE, L)
    tk = min(2048, L)
    kb = L // tk
    off = row_off // tk
    return _matmul(a, y, out_shape=(nb * L, FNET_COLS), out_dtype=F32, tm=tm, tn=FNET_COLS, tk=tk,
                   grid=(nb, L // tm, 1, 2 * kb),
                   a_map=lambda b, i, u, k: (i, k),
                   w_map=lambda b, i, u, k: (off + b * kb + k % kb, k // kb),
                   o_map=lambda b, i, u, k: (b * (L // tm) + i, 0),
                   name=f"fnet_pos_{L}")


FFT_MINOR = 128
FFT_GROUP = 8
FFT_COLS = 8


def _fft_major_matrix(n1):
    k = jnp.arange(n1, dtype=jnp.int32)
    ang = (2.0 * math.pi / n1) * ((k[:, None] * k[None, :]) % n1).astype(F32)
    return jnp.concatenate([jnp.cos(ang), jnp.sin(ang)], axis=0).astype(BF16)


def _fft_minor_matrices(L):
    n2c, grp = FFT_MINOR, FFT_GROUP
    n1 = L // n2c
    g = jnp.arange(n1 // grp, dtype=jnp.int32)[:, None, None, None, None]
    k2 = jnp.arange(n2c, dtype=jnp.int32)[None, :, None, None, None]
    j = jnp.arange(grp, dtype=jnp.int32)[None, None, :, None, None]
    jp = jnp.arange(grp, dtype=jnp.int32)[None, None, None, :, None]
    n2 = jnp.arange(n2c, dtype=jnp.int32)[None, None, None, None, :]
    phase = (n2 * k2 * n1 + n2 * (grp * g + j)) % L
    ang = (2.0 * math.pi / L) * phase.astype(F32)
    keep = (j == jp).astype(F32) * (L ** -0.5)
    shape = (n1 // grp, n2c * grp, grp * n2c)
    return ((jnp.cos(ang) * keep).reshape(shape).astype(BF16), (jnp.sin(ang) * keep).reshape(shape).astype(BF16))


def _fft_major_kernel(cs_ref, y_ref, a_ref, *, n1):
    pq = jnp.dot(cs_ref[...], y_ref[...], preferred_element_type=F32)
    p, q = pq[:n1], pq[n1:]
    w = FNET_COLS
    parts = []
    for t in range(FFT_COLS):
        re, im = slice(2 * t * w, (2 * t + 1) * w), slice((2 * t + 1) * w, (2 * t + 2) * w)
        parts += [p[:, re] - q[:, im], -(p[:, im] + q[:, re])]
    a_ref[...] = jnp.concatenate(parts, axis=1).astype(BF16)


def _fft_minor_kernel(mc_ref, ms_ref, a_ref, o_ref):
    w = FNET_COLS
    out = (jnp.dot(mc_ref[...], a_ref[:, :w], preferred_element_type=F32)
           + jnp.dot(ms_ref[...], a_ref[:, w:], preferred_element_type=F32))
    o_ref[...] = out.reshape(o_ref.shape)


def _fnet_positions_fft(y, nb, L):
    n2c, grp = FFT_MINOR, FFT_GROUP
    n1 = L // n2c
    w2 = 2 * FNET_COLS
    cs = _fft_major_matrix(n1)
    mc, ms = _fft_minor_matrices(L)
    flat = y[:nb * L].reshape(nb * n1, n2c * w2)
    tc = FFT_COLS * w2
    a = pl.pallas_call(
        functools.partial(_fft_major_kernel, n1=n1),
        grid=(nb, n2c // FFT_COLS),
        in_specs=[pl.BlockSpec((2 * n1, n1), lambda b, t: (0, 0)), pl.BlockSpec((n1, tc), lambda b, t: (b, t))],
        out_specs=pl.BlockSpec((n1, tc), lambda b, t: (b, t)),
        out_shape=jax.ShapeDtypeStruct((nb * n1, n2c * w2), BF16),
        compiler_params=_params("arbitrary", "arbitrary"),
        name="fft_major",
    )(cs, flat)
    a = a.reshape(nb * L, w2)
    rows = grp * n2c
    ng = n1 // grp
    out = pl.pallas_call(
        _fft_minor_kernel,
        grid=(ng, nb),
        in_specs=[pl.BlockSpec((None, rows, rows), lambda g, b: (g, 0, 0)),
                  pl.BlockSpec((None, rows, rows), lambda g, b: (g, 0, 0)),
                  pl.BlockSpec((rows, w2), lambda g, b: (b * ng + g, 0))],
        out_specs=pl.BlockSpec((None, n2c, grp, FNET_COLS), lambda g, b: (b, 0, g, 0)),
        out_shape=jax.ShapeDtypeStruct((nb, n2c, n1, FNET_COLS), F32),
        compiler_params=_params("arbitrary", "arbitrary"),
        name="fft_minor",
    )(mc, ms, a)
    return out.reshape(nb * L, FNET_COLS)


def _merge_kernel(*refs, lat_tiles):
    b_refs = refs[:2 * N_BRANCH]
    g_ref, wb_ref, wo_ref, o_ref = refs[2 * N_BRANCH:]
    D = D_MODEL
    is_ctx = pl.program_id(0) >= lat_tiles
    m = None
    for n in range(N_BRANCH):
        lat_ref, ctx_ref = b_refs[2 * n], b_refs[2 * n + 1]
        b = jnp.where(is_ctx, ctx_ref[...].astype(BF16), lat_ref[...].astype(BF16))
        proj = jnp.dot(b, wb_ref[n], preferred_element_type=F32)
        t = jax.nn.sigmoid(g_ref[:, n * D:(n + 1) * D].astype(F32)) * proj
        m = t if m is None else m + t
    o_ref[...] = jnp.dot(m.astype(BF16), wo_ref[...], preferred_element_type=F32)


def _merge(branches, gates, w_branch, w_out):
    R = gates.shape[0]
    D = D_MODEL
    tm = MERGE_TILE
    lat_tiles = branches[0][0].shape[0] // tm
    ctx_tiles = branches[0][1].shape[0] // tm
    lat_spec = pl.BlockSpec((tm, BRANCH_W), lambda i: (jnp.minimum(i, lat_tiles - 1), 0))
    ctx_spec = pl.BlockSpec((tm, BRANCH_W), lambda i: (jnp.clip(i - lat_tiles, 0, ctx_tiles - 1), 0))
    specs = [lat_spec, ctx_spec] * N_BRANCH
    branches = [a for pair in branches for a in pair]
    specs += [pl.BlockSpec((tm, N_BRANCH * D), lambda i: (i, 0)),
              pl.BlockSpec((N_BRANCH, BRANCH_W, D), lambda i: (0, 0, 0)),
              pl.BlockSpec((D, D), lambda i: (0, 0))]
    return pl.pallas_call(
        functools.partial(_merge_kernel, lat_tiles=lat_tiles),
        grid=(R // tm,),
        in_specs=specs,
        out_specs=pl.BlockSpec((tm, D), lambda i: (i, 0)),
        out_shape=jax.ShapeDtypeStruct((R, D), F32),
        compiler_params=_params("arbitrary"),
        name="merge",
    )(*branches, gates, w_branch, w_out)


def _moe_plan(rec, counts, R):
    T = MOE_TILE
    cnt = counts[:, 0].astype(jnp.int32)
    padded = ((cnt + T - 1) // T) * T
    ends = jnp.cumsum(padded)
    base = ends - padded
    n_tiles = (2 * R) // T + N_EXPERTS
    n_used = (ends[-1] // T).astype(jnp.int32)
    tile = jnp.minimum(jnp.arange(n_tiles, dtype=jnp.int32), n_used - 1)
    tile_expert = jnp.sum((ends // T)[None, :] <= tile[:, None], axis=1).astype(jnp.int32)
    tile_expert = jnp.minimum(tile_expert, N_EXPERTS - 1)
    e_ab = rec[0:2].astype(jnp.int32)
    r_ab = rec[2:4].astype(jnp.int32)
    base_ab = jnp.zeros_like(e_ab)
    for e in range(N_EXPERTS):
        base_ab = jnp.where(e_ab == e, base[e], base_ab)
    pos = (base_ab + r_ab - 1).T.reshape(2 * R)
    wgt = rec[4:6].T
    tail = jnp.concatenate([ends - T, (cnt > 0).astype(jnp.int32), n_used.reshape(1)]).astype(jnp.int32)
    return pos, wgt, tile_expert, n_used.reshape(1), tail


def _row_copy(src, src_row, dst, dst_row, sem):
    return pltpu.make_async_copy(src.at[pl.ds(src_row, 1)], dst.at[pl.ds(dst_row, 1)], sem)


def _dispatch_kernel(pos_ref, tail_ref, h_ref, buf_ref, zero_ref, sem, zsem, *, tm, n_tiles):
    i = pl.program_id(0)

    @pl.when(i == 0)
    def _():
        zero_ref[...] = jnp.zeros_like(zero_ref)

        def tail_copy(e):
            row = pl.multiple_of(tail_ref[e], MOE_TILE)
            return pltpu.make_async_copy(zero_ref, buf_ref.at[pl.ds(row, MOE_TILE)], zsem)

        for e in range(N_EXPERTS):
            @pl.when(tail_ref[N_EXPERTS + e] > 0)
            def _():
                tail_copy(e).start()
        for e in range(N_EXPERTS):
            @pl.when(tail_ref[N_EXPERTS + e] > 0)
            def _():
                tail_copy(e).wait()

        def unused_copy(t):
            return pltpu.make_async_copy(zero_ref, buf_ref.at[pl.ds(pl.multiple_of(t * MOE_TILE, MOE_TILE), MOE_TILE)],
                                         zsem)

        def clear(t, carry):
            unused_copy(t).start()
            unused_copy(t).wait()
            return carry

        lax.fori_loop(tail_ref[2 * N_EXPERTS], n_tiles, clear, 0)

    for t in range(tm):
        for slot in range(2):
            _row_copy(h_ref, t, buf_ref, pos_ref[(i * tm + t) * 2 + slot], sem).start()
    for slot in range(2):
        pltpu.make_async_copy(h_ref, buf_ref.at[pl.ds(0, tm)], sem).wait()


def _moe_dispatch(h, pos, tail, n_rows):
    R, D = h.shape
    tm = ROW_TILE
    return pl.pallas_call(
        functools.partial(_dispatch_kernel, tm=tm, n_tiles=n_rows // MOE_TILE),
        grid_spec=pltpu.PrefetchScalarGridSpec(
            num_scalar_prefetch=2,
            grid=(R // tm,),
            in_specs=[pl.BlockSpec((tm, D), lambda i, pos, tail: (i, 0))],
            out_specs=pl.BlockSpec(memory_space=pl.ANY),
            scratch_shapes=[pltpu.VMEM((MOE_TILE, D), F32), pltpu.SemaphoreType.DMA, pltpu.SemaphoreType.DMA],
        ),
        out_shape=jax.ShapeDtypeStruct((n_rows, D), F32),
        compiler_params=_params("arbitrary"),
        name="moe_dispatch",
    )(pos, tail, h)


def _experts_kernel(te_ref, nu_ref, x_ref, w1_ref, w3_ref, w2_ref, o_ref):
    @pl.when(pl.program_id(0) >= nu_ref[0])
    def _():
        o_ref[...] = jnp.zeros_like(o_ref)

    @pl.when(pl.program_id(0) < nu_ref[0])
    def _():
        x = x_ref[...].astype(BF16)
        a = jnp.dot(x, w1_ref[...], preferred_element_type=F32)
        b = jnp.dot(x, w3_ref[...], preferred_element_type=F32)
        hid = (a * jax.nn.sigmoid(a) * b).astype(BF16)
        o_ref[...] = jnp.dot(hid, w2_ref[...], preferred_element_type=F32)


def _moe_experts(xs, tile_expert, n_used, w1, w3, w2):
    P, D = xs.shape
    T = MOE_TILE
    row = pl.BlockSpec((T, D), lambda i, te, nu: (i, 0))
    return pl.pallas_call(
        _experts_kernel,
        grid_spec=pltpu.PrefetchScalarGridSpec(
            num_scalar_prefetch=2,
            grid=(P // T,),
            in_specs=[row,
                      pl.BlockSpec((None, D, D_FF), lambda i, te, nu: (te[i], 0, 0)),
                      pl.BlockSpec((None, D, D_FF), lambda i, te, nu: (te[i], 0, 0)),
                      pl.BlockSpec((None, D_FF, D), lambda i, te, nu: (te[i], 0, 0))],
            out_specs=row,
        ),
        out_shape=jax.ShapeDtypeStruct((P, D), F32),
        compiler_params=_params("arbitrary"),
        name="moe_experts",
    )(tile_expert, n_used, xs, w1, w3, w2)


def _combine_norm_kernel(*refs, tc, n_tiles, alpha, emit_h):
    refs = list(refs)
    pos_ref, x_ref, w_ref, g_ref, lng_ref, lnb_ref = refs[:6]
    refs = refs[6:]
    if emit_h:
        sc_ref, sh_ref = refs[:2]
        refs = refs[2:]
    y_ref, xo_ref = refs[:2]
    refs = refs[2:]
    if emit_h:
        h_ref = refs.pop(0)
    bufs = (refs[0:2], refs[2:4])
    sem = refs[4]
    i = pl.program_id(0)

    def issue(tile, slot, t):
        p = (tile * tc + t) * 2
        _row_copy(y_ref, pos_ref[p], bufs[slot][0], t, sem.at[slot]).start()
        _row_copy(y_ref, pos_ref[p + 1], bufs[slot][1], t, sem.at[slot]).start()

    def wait(slot):
        for buf in bufs[slot]:
            pltpu.make_async_copy(y_ref.at[pl.ds(0, tc)], buf, sem.at[slot]).wait()

    @pl.when(i == 0)
    def _():
        lax.fori_loop(0, tc, lambda t, c: (issue(0, 0, t), c)[1], 0, unroll=8)

    nxt = jnp.minimum(i + 1, n_tiles - 1)
    for slot in range(2):
        @pl.when(i % 2 == slot)
        def _(slot=slot):
            wait(slot)
            for t in range(tc):
                issue(nxt, 1 - slot, t)
            ya_ref, yb_ref = bufs[slot]
            w = w_ref[...]
            f = w[:, 0:1] * ya_ref[...] + w[:, 1:2] * yb_ref[...]
            xn = _standardize(alpha * x_ref[...] + g_ref[...] * f) * lng_ref[...] + lnb_ref[...]
            xo_ref[...] = xn
            if emit_h:
                h_ref[...] = (xn * (1.0 + sc_ref[...]) + sh_ref[...]).astype(BF16)

            @pl.when(i == n_tiles - 1)
            def _():
                wait(1 - slot)


def _moe_combine_norm(x, ys, pos, wgt, *, seg_of_row, alpha, mod_g, k_gate, ln_g, ln_b, mod_h=None, k_scale=None,
                      k_shift=None, name):
    R, D = x.shape
    tc = COMBINE_TILE
    emit_h = mod_h is not None
    row = pl.BlockSpec((tc, D), lambda i, pos: (i, 0))

    def mod_spec(k):
        return pl.BlockSpec((None, 1, D), lambda i, pos: (seg_of_row(i * tc) * 6 + k, 0, 0))

    vec = pl.BlockSpec((1, D), lambda i, pos: (0, 0))
    args = [x, wgt, mod_g, ln_g.reshape(1, D), ln_b.reshape(1, D)]
    specs = [row, pl.BlockSpec((tc, 2), lambda i, pos: (i, 0)), mod_spec(k_gate), vec, vec]
    if emit_h:
        args += [mod_h, mod_h]
        specs += [mod_spec(k_scale), mod_spec(k_shift)]
    args.append(ys)
    specs.append(pl.BlockSpec(memory_space=pl.ANY))
    out_shape = [jax.ShapeDtypeStruct((R, D), F32)]
    out_specs = [row]
    if emit_h:
        out_shape.append(jax.ShapeDtypeStruct((R, D), BF16))
        out_specs.append(row)
    return pl.pallas_call(
        functools.partial(_combine_norm_kernel, tc=tc, n_tiles=R // tc, alpha=alpha, emit_h=emit_h),
        grid_spec=pltpu.PrefetchScalarGridSpec(
            num_scalar_prefetch=1,
            grid=(R // tc,),
            in_specs=specs,
            out_specs=out_specs,
            scratch_shapes=[pltpu.VMEM((tc, D), F32)] * 4 + [pltpu.SemaphoreType.DMA((2,))],
        ),
        out_shape=out_shape,
        compiler_params=_params("arbitrary"),
        name=name,
    )(pos, *args)


def _pad_heads(w, heads, width, lo, hi):
    lead = w.shape[:-1]
    w = w.reshape(lead + (heads, width))[..., lo:hi]
    w = jnp.pad(w, ((0, 0),) * (len(lead) + 1) + ((0, HEAD_PAD - (hi - lo)),))
    return w.reshape(lead + (heads * HEAD_PAD,))


def kernel(x, c, ctx, c_ctx, w_ada, b_ada, w_in, mla_q_norm, mla_w_uq, mla_kv_norm, mla_w_ukv, diff_lq1, diff_lk1,
           diff_lq2, diff_lk2, diff_subln, conv_w, conv_b, conv_ln_g, conv_ln_b, w_branch, w_out, ln1_g, ln1_b,
           ln2_g, ln2_b, router_w, router_bias, exp_w1, exp_w3, exp_w2):
    B, S, D = x.shape
    C = ctx.shape[1]
    depth = w_in.shape[0]
    N, NC = B * S, B * C
    tm = ROW_TILE
    assert D == D_MODEL and S % tm == 0 and NC % tm == 0 and S % C == 0 and S % GRID_W == 0

    R = N + NC

    def seg_of_row(r):
        return jnp.minimum(r // S, B)

    def seg(i):
        return seg_of_row(i * tm)

    def pos_block(i):
        return jnp.where(i < N // tm, i % (S // tm), S // tm)

    rows = S // GRID_W
    tabs_mq = _rope_tables(rows, MLA_DR, MLA_DN, HEAD_PAD, MLA_HEADS, tm)
    tabs_mk = _rope_tables(rows, MLA_DR, 0, LANES, 1, tm)
    tabs_d = _rope_tables(rows, DIFF_D, 0, DIFF_D, 2 * DIFF_HEADS, tm)

    nseg = 8
    cond = jnp.zeros((nseg, D), F32).at[:B].set(c).at[B].set(c_ctx)
    mod_all = _ada(cond, w_ada, b_ada).reshape(depth, nseg * 6, 1, D)

    dft_ch = _dft_channel_matrix().astype(BF16)
    dft_ctx = _dft_position_matrix(C).astype(BF16)
    if S % (FFT_MINOR * FFT_GROUP) == 0:
        fnet_lat = lambda y: _fnet_positions_fft(y, B, S)
    else:
        dft_lat = _dft_position_matrix(S).astype(BF16)
        fnet_lat = lambda y: _fnet_positions(y, dft_lat, 0, B, S)
    rw_t = router_w.T
    rw_hi = rw_t.astype(BF16)
    rw_lo = (rw_t - rw_hi.astype(F32)).astype(BF16)
    router = (rw_hi, rw_lo, router_bias.reshape(N_EXPERTS, 1))

    xs = jnp.concatenate([x.reshape(N, D), ctx.reshape(NC, D)], axis=0)
    alpha = (2.0 * depth) ** 0.25
    xs, h = _norm(xs, seg=seg, alpha=alpha, first=True, mod_h=mod_all[0], k_scale=1, k_shift=0, name="norm_in")

    w_mla = jnp.pad(w_in[:, :, :OFF_DIFF], ((0, 0), (0, 0), (0, MLA_COLS_PAD - MLA_COLS))).astype(BF16)
    w_diff = w_in[:, :, OFF_DIFF:OFF_CONV].astype(BF16)
    w_conv = w_in[:, :, OFF_CONV:OFF_FNET].astype(BF16)
    w_fnet = w_in[:, :, OFF_FNET:OFF_GATE].astype(BF16)
    w_gate = w_in[:, :, OFF_GATE:].astype(BF16)
    wq = _pad_heads(mla_w_uq, MLA_HEADS, MLA_DN + MLA_DR, 0, MLA_DN + MLA_DR).astype(BF16)
    wk = _pad_heads(mla_w_ukv, MLA_HEADS, MLA_DN + MLA_DV, 0, MLA_DN).astype(BF16)
    wv = _pad_heads(mla_w_ukv, MLA_HEADS, MLA_DN + MLA_DV, MLA_DN, MLA_DN + MLA_DV).astype(BF16)

    for l in range(depth):
        mod = mod_all[l]
        cm = _project(h, w_mla, F32, "proj_mla", MLA_COLS_PAD, layer=l)
        cd = _project(h, w_diff, BF16, "proj_diff", DIFF_COLS, layer=l)
        cc = _project(h, w_conv, BF16, "proj_conv", CONV_COLS, layer=l)
        cf = _project(h, w_fnet, BF16, "proj_fnet", FNET_COLS, layer=l)
        gates = _project(h, w_gate, BF16, "proj_gate", 2048, layer=l)

        mq, mk, mv = _mla_prep(cm, mla_q_norm[l].reshape(1, -1), mla_kv_norm[l].reshape(1, -1), wq[l], wk[l], wv[l],
                               tabs_mq, tabs_mk, pos_block)
        o_mla = (_attention("mla", mq, mk, mv, B=B, S=S, C=C, ctx_queries=False),
                 _attention("mla", mq, mk, mv, B=B, S=S, C=C, ctx_queries=True))

        lam_init = 0.8 - 0.6 * math.exp(-0.3 * l)
        ll = jnp.stack([diff_lq1[l], diff_lk1[l], diff_lq2[l], diff_lk2[l]], axis=0)
        sub = diff_subln[l].reshape(1, -1)
        dq, dk, dv = _diff_prep(cd, tabs_d, pos_block)
        o_diff = (_attention("diff", dq, dk, dv, B=B, S=S, C=C, ctx_queries=False, extra=(ll, sub), lam_init=lam_init),
                  _attention("diff", dq, dk, dv, B=B, S=S, C=C, ctx_queries=True, extra=(ll, sub), lam_init=lam_init))

        o_conv = (_conv(cc, 0, B, S, conv_w[l], conv_b[l], conv_ln_g[l], conv_ln_b[l]),
                  _conv(cc, N, B, C, conv_w[l], conv_b[l], conv_ln_g[l], conv_ln_b[l]))
        y = _project(cf, dft_ch, BF16, "fnet_ch", 2 * FNET_COLS)
        o_fnet = (fnet_lat(y), _fnet_positions(y, dft_ctx, N, B, C))

        mix = _merge([o_mla, o_diff, o_conv, o_fnet], gates, w_branch[l].astype(BF16), w_out[l].astype(BF16))
        xs, h2, rec, counts = _norm(xs, seg=seg, alpha=alpha, m=mix, mod_g=mod, k_gate=2, ln_g=ln1_g[l],
                                    ln_b=ln1_b[l], mod_h=mod, k_scale=4, k_shift=3, router=router, name="norm1")
        pos, wgt, tile_expert, n_used, tail = _moe_plan(rec, counts, R)
        sorted_rows = _moe_dispatch(h2, pos, tail, tile_expert.shape[0] * MOE_TILE)
        ys = _moe_experts(sorted_rows, tile_expert, n_used, exp_w1[l].astype(BF16), exp_w3[l].astype(BF16),
                          exp_w2[l].astype(BF16))
        last = l + 1 == depth
        out = _moe_combine_norm(xs, ys, pos, wgt, seg_of_row=seg_of_row, alpha=alpha, mod_g=mod, k_gate=5,
                                ln_g=ln2_g[l], ln_b=ln2_b[l], mod_h=None if last else mod_all[l + 1],
                                k_scale=1, k_shift=0, name="norm_out" if last else "norm2")
        if last:
            (xs,) = out
        else:
            xs, h = out
    return xs[:N].reshape(B, S, D)
```

```python
import functools
import math

import jax
import jax.numpy as jnp
from jax import lax
from jax.experimental import pallas as pl
from jax.experimental.pallas import tpu as pltpu

F32 = jnp.float32
BF16 = jnp.bfloat16

D_MODEL = 2048
GRID_W = 64
ROPE_THETA = 10000.0
EPS = 1e-6

MLA_HEADS = 8
MLA_DN = 64
MLA_DR = 32
MLA_DV = 64
MLA_Q_RANK = 384
MLA_KV_RANK = 256
DIFF_HEADS = 4
DIFF_D = 64
CONV_W = 512
CONV_K = 31
FNET_GROUPS = 4
FNET_GROUP_W = 128
N_BRANCH = 4
BRANCH_W = 512
N_EXPERTS = 16
N_GROUPS = 4
EXPERTS_PER_GROUP = N_EXPERTS // N_GROUPS
D_FF = 1024

MLA_COLS = MLA_Q_RANK + MLA_KV_RANK + MLA_DR
DIFF_COLS = 3 * DIFF_HEADS * 2 * DIFF_D
CONV_COLS = 2 * CONV_W
FNET_COLS = FNET_GROUPS * FNET_GROUP_W
GATE_COLS = N_BRANCH * D_MODEL
OFF_DIFF = MLA_COLS
OFF_CONV = OFF_DIFF + DIFF_COLS
OFF_FNET = OFF_CONV + CONV_COLS
OFF_GATE = OFF_FNET + FNET_COLS

LANES = 128
SUBLANES = 8
HEAD_PAD = 128
MLA_COLS_PAD = 768
CONV_HALO = 16
VMEM_LIMIT = 56 * 1024 * 1024
LOG2E = math.log2(math.e)

ROW_TILE = 512
MERGE_TILE = 256
MOE_TILE = 512
COMBINE_TILE = 256
ATTN_TQ = 512
ATTN_TK = {"mla": 2048, "diff": 1024}


def _params(*sem):
    return pltpu.CompilerParams(dimension_semantics=("arbitrary",) * len(sem), vmem_limit_bytes=VMEM_LIMIT)


def _mm_kernel(a_ref, w_ref, o_ref, acc_ref, *, nk):
    prod = jnp.dot(a_ref[...], w_ref[...], preferred_element_type=F32)
    if nk == 1:
        o_ref[...] = prod.astype(o_ref.dtype)
        return
    k = pl.program_id(3)

    @pl.when(k == 0)
    def _():
        acc_ref[...] = prod

    @pl.when(k > 0)
    def _():
        acc_ref[...] += prod

    @pl.when(k == nk - 1)
    def _():
        o_ref[...] = acc_ref[...].astype(o_ref.dtype)


def _matmul(a, w, *, out_shape, out_dtype, tm, tn, tk, grid, a_map, o_map, name, w_map=None, w_spec=None):
    nk = grid[3]
    if w_spec is None:
        w_spec = pl.BlockSpec((tk, tn), w_map)
    return pl.pallas_call(
        functools.partial(_mm_kernel, nk=nk),
        grid=grid,
        in_specs=[pl.BlockSpec((tm, tk), a_map), w_spec],
        out_specs=pl.BlockSpec((tm, tn), o_map),
        out_shape=jax.ShapeDtypeStruct(out_shape, out_dtype),
        scratch_shapes=[pltpu.VMEM((tm, tn), F32)],
        compiler_params=_params("parallel", "parallel", "parallel", "arbitrary"),
        name=name,
    )(a, w)


def _project(h, w, out_dtype, name, tn, layer=None, col0=0, n_cols=None, tm=ROW_TILE):
    R, K = h.shape
    N = w.shape[-1] if n_cols is None else n_cols
    assert col0 % tn == 0 and N % tn == 0 and R % tm == 0
    c0 = col0 // tn
    if layer is None:
        w_spec = pl.BlockSpec((K, tn), lambda i, j, u, k: (0, c0 + j))
    else:
        w_spec = pl.BlockSpec((None, K, tn), lambda i, j, u, k: (layer, 0, c0 + j))
    return _matmul(h, w, out_shape=(R, N), out_dtype=out_dtype, tm=tm, tn=tn, tk=K,
                   grid=(R // tm, N // tn, 1, 1),
                   a_map=lambda i, j, u, k: (i, 0), w_spec=w_spec,
                   o_map=lambda i, j, u, k: (i, j), name=name)


def _ada_kernel(c_ref, w_ref, b_ref, o_ref):
    c = c_ref[...]
    a = (c * jax.nn.sigmoid(c)).astype(BF16)
    o_ref[...] = jnp.dot(a, w_ref[...].astype(BF16), preferred_element_type=F32) + b_ref[...]


def _ada(cond, w_ada, b_ada):
    L, D, N6 = w_ada.shape
    rows = cond.shape[0]
    tn = 1024
    return pl.pallas_call(
        _ada_kernel,
        grid=(L, N6 // tn),
        in_specs=[pl.BlockSpec((rows, D), lambda l, j: (0, 0)),
                  pl.BlockSpec((None, D, tn), lambda l, j: (l, 0, j)),
                  pl.BlockSpec((None, 1, tn), lambda l, j: (l, 0, j))],
        out_specs=pl.BlockSpec((None, rows, tn), lambda l, j: (l, 0, j)),
        out_shape=jax.ShapeDtypeStruct((L, rows, N6), F32),
        compiler_params=_params("parallel", "parallel"),
        name="ada_mod",
    )(cond, w_ada, b_ada.reshape(L, 1, N6))


def _route(logits_t, bias_t):
    scores = jax.nn.sigmoid(logits_t)
    sel = scores + bias_t
    sel_r = [sel[e:e + 1, :] for e in range(N_EXPERTS)]
    sc_r = [scores[e:e + 1, :] for e in range(N_EXPERTS)]
    per = EXPERTS_PER_GROUP
    gscore = []
    for g in range(N_GROUPS):
        r = sel_r[g * per:(g + 1) * per]
        best = None
        for a in range(per):
            for b in range(a + 1, per):
                s = r[a] + r[b]
                best = s if best is None else jnp.maximum(best, s)
        gscore.append(best)
    best_g = jnp.zeros_like(gscore[0], dtype=jnp.int32)
    best_v = gscore[0]
    for g in range(1, N_GROUPS):
        better = gscore[g] > best_v
        best_g = jnp.where(better, g, best_g)
        best_v = jnp.where(better, gscore[g], best_v)
    v = []
    s = []
    for j in range(per):
        vj = sel_r[j]
        sj = sc_r[j]
        for g in range(1, N_GROUPS):
            vj = jnp.where(best_g == g, sel_r[g * per + j], vj)
            sj = jnp.where(best_g == g, sc_r[g * per + j], sj)
        v.append(vj)
        s.append(sj)
    chosen = []
    for j in range(per):
        rank = jnp.zeros_like(best_g)
        for i in range(per):
            if i == j:
                continue
            ahead = (v[i] > v[j]) | ((v[i] == v[j]) & (i < j))
            rank = rank + ahead.astype(jnp.int32)
        chosen.append(rank < 2)
    total = jnp.zeros_like(s[0])
    for j in range(per):
        total = total + jnp.where(chosen[j], s[j], 0.0)
    picked, weight = [], []
    for e in range(N_EXPERTS):
        g, j = divmod(e, per)
        on = (best_g == g) & chosen[j]
        picked.append(on)
        weight.append(jnp.where(on, s[j] / total, 0.0))
    return picked, weight


def _dispatch_info(picked, weight, tri_ref, tot_ref):
    sel = jnp.concatenate([p.astype(F32) for p in picked], axis=0)
    cum = jnp.dot(sel.astype(BF16), tri_ref[...], preferred_element_type=F32)
    tot = tot_ref[...]
    rank = cum + tot[:, 0:1]
    tot_ref[...] = tot + jnp.sum(sel, axis=1, keepdims=True)
    zero = jnp.zeros_like(weight[0])
    seen = zero
    rec = [zero] * 6
    for e in range(N_EXPERTS):
        on = picked[e]
        for slot, is_slot in ((0, on & (seen == 0.0)), (1, on & (seen == 1.0))):
            rec[slot] = jnp.where(is_slot, float(e), rec[slot])
            rec[2 + slot] = jnp.where(is_slot, rank[e:e + 1, :], rec[2 + slot])
            rec[4 + slot] = jnp.where(is_slot, weight[e], rec[4 + slot])
        seen = seen + on.astype(F32)
    return jnp.concatenate(rec + [zero, zero], axis=0)


def _standardize(y):
    mu = jnp.mean(y, axis=-1, keepdims=True)
    d = y - mu
    var = jnp.mean(d * d, axis=-1, keepdims=True)
    return d * lax.rsqrt(var + EPS)


def _norm_kernel(*refs, first, emit_h, router, alpha):
    refs = list(refs)
    x_ref = refs.pop(0)
    if not first:
        m_ref, g_ref, lng_ref, lnb_ref = refs[:4]
        refs = refs[4:]
    if emit_h:
        sc_ref, sh_ref = refs[:2]
        refs = refs[2:]
    if router:
        rwh_ref, rwl_ref, rb_ref, tri_ref = refs[:4]
        refs = refs[4:]
    xo_ref = refs.pop(0)
    x = x_ref[...]
    xn = _standardize(x if first else alpha * x + g_ref[...] * m_ref[...])
    if not first:
        xn = xn * lng_ref[...] + lnb_ref[...]
    xo_ref[...] = xn
    if not emit_h:
        return
    h_ref = refs.pop(0)
    h = xn * (1.0 + sc_ref[...]) + sh_ref[...]
    h_ref[...] = h.astype(h_ref.dtype)
    if router:
        rec_ref, cnt_ref, tot_ref = refs

        @pl.when(pl.program_id(0) == 0)
        def _():
            tot_ref[...] = jnp.zeros_like(tot_ref)

        h_hi = h.astype(BF16)
        h_lo = (h - h_hi.astype(F32)).astype(BF16)
        nt = (((1,), (1,)), ((), ()))
        logits_t = (lax.dot_general(rwh_ref[...], h_hi, nt, preferred_element_type=F32)
                    + lax.dot_general(rwh_ref[...], h_lo, nt, preferred_element_type=F32)
                    + lax.dot_general(rwl_ref[...], h_hi, nt, preferred_element_type=F32))
        picked, weight = _route(logits_t, rb_ref[...])
        rec_ref[...] = _dispatch_info(picked, weight, tri_ref, tot_ref)
        cnt_ref[...] = tot_ref[...]


def _norm(x, *, seg, alpha, first=False, m=None, mod_g=None, k_gate=None, ln_g=None, ln_b=None,
          mod_h=None, k_scale=None, k_shift=None, router=None, name):
    R, D = x.shape
    tm = ROW_TILE
    emit_h = mod_h is not None
    row = pl.BlockSpec((tm, D), lambda i: (i, 0))

    def mod_spec(k):
        return pl.BlockSpec((None, 1, D), lambda i: (seg(i) * 6 + k, 0, 0))

    vec = pl.BlockSpec((1, D), lambda i: (0, 0))
    args, specs = [x], [row]
    if not first:
        args += [m, mod_g, ln_g.reshape(1, D), ln_b.reshape(1, D)]
        specs += [row, mod_spec(k_gate), vec, vec]
    if emit_h:
        args += [mod_h, mod_h]
        specs += [mod_spec(k_scale), mod_spec(k_shift)]
    scratch = []
    if router is not None:
        rwh, rwl, rb = router
        tri = (jnp.arange(tm)[:, None] <= jnp.arange(tm)[None, :]).astype(BF16)
        args += [rwh, rwl, rb, tri]
        specs += [pl.BlockSpec((N_EXPERTS, D), lambda i: (0, 0))] * 2
        specs += [pl.BlockSpec((N_EXPERTS, 1), lambda i: (0, 0)), pl.BlockSpec((tm, tm), lambda i: (0, 0))]
        scratch = [pltpu.VMEM((N_EXPERTS, LANES), F32)]
    out_shape = [jax.ShapeDtypeStruct((R, D), F32)]
    out_specs = [row]
    if emit_h:
        out_shape.append(jax.ShapeDtypeStruct((R, D), BF16 if router is None else F32))
        out_specs.append(row)
    if router is not None:
        out_shape += [jax.ShapeDtypeStruct((8, R), F32), jax.ShapeDtypeStruct((N_EXPERTS, LANES), F32)]
        out_specs += [pl.BlockSpec((8, tm), lambda i: (0, i)), pl.BlockSpec((N_EXPERTS, LANES), lambda i: (0, 0))]
    return pl.pallas_call(
        functools.partial(_norm_kernel, first=first, emit_h=emit_h, router=router is not None, alpha=alpha),
        grid=(R // tm,),
        in_specs=specs,
        out_specs=out_specs,
        out_shape=out_shape,
        scratch_shapes=scratch,
        compiler_params=_params("arbitrary" if router is not None else "parallel"),
        name=name,
    )(*args)


def _rope_tables(rows, rot_dim, lane0, chunk, reps, ident_rows):
    n = rows * GRID_W
    r = jnp.broadcast_to(jnp.arange(rows, dtype=F32)[:, None], (rows, GRID_W)).reshape(n)
    c = jnp.broadcast_to(jnp.arange(GRID_W, dtype=F32)[None, :], (rows, GRID_W)).reshape(n)
    quarter = rot_dim // 4
    inv_freq = ROPE_THETA ** (-jnp.arange(quarter, dtype=F32) / quarter)
    ar = r[:, None] * inv_freq
    ac = c[:, None] * inv_freq
    ang = jnp.concatenate([ar, ar, ac, ac], -1)
    sign = jnp.concatenate([-jnp.ones((quarter,), F32), jnp.ones((quarter,), F32)] * 2)
    cos = jnp.ones((n, chunk), F32).at[:, lane0:lane0 + rot_dim].set(jnp.cos(ang))
    sin = jnp.zeros((n, chunk), F32).at[:, lane0:lane0 + rot_dim].set(jnp.sin(ang) * sign)
    cos = jnp.concatenate([cos, jnp.ones((ident_rows, chunk), F32)], 0)
    sin = jnp.concatenate([sin, jnp.zeros((ident_rows, chunk), F32)], 0)
    return jnp.tile(cos, (1, reps)), jnp.tile(sin, (1, reps))


def _rope(x, cos, sin_signed, quarter, lane0):
    w = x.shape[-1]
    lane = lax.broadcasted_iota(jnp.int32, x.shape, 1)
    even = (((lane - lane0) // quarter) % 2) == 0
    partner = jnp.where(even, pltpu.roll(x, w - quarter, 1), pltpu.roll(x, quarter, 1))
    return x * cos + partner * sin_signed


def _rms(x, g):
    return x * lax.rsqrt(jnp.mean(x * x, axis=-1, keepdims=True) + EPS) * g


def _mla_prep_kernel(cm_ref, qn_ref, kvn_ref, wq_ref, wk_ref, wv_ref, cq_ref, sq_ref, ck_ref, sk_ref,
                     q_ref, k_ref, v_ref):
    cm = cm_ref[...]
    cq = cm[:, :MLA_Q_RANK]
    ckv = cm[:, MLA_Q_RANK:MLA_Q_RANK + MLA_KV_RANK]
    kr = cm[:, MLA_Q_RANK + MLA_KV_RANK:]
    qn = _rms(cq, qn_ref[...]).astype(BF16)
    kvn = _rms(ckv, kvn_ref[...]).astype(BF16)
    q = jnp.dot(qn, wq_ref[...], preferred_element_type=F32)
    q = _rope(q, cq_ref[...], sq_ref[...], MLA_DR // 4, MLA_DN)
    scale = (MLA_DN + MLA_DR) ** -0.5 * LOG2E
    q_ref[...] = (q * scale).astype(BF16)
    kr = _rope(kr, ck_ref[...], sk_ref[...], MLA_DR // 4, 0)
    kr = pltpu.roll(kr, MLA_DN, 1)
    k = jnp.dot(kvn, wk_ref[...], preferred_element_type=F32)
    k_ref[...] = (k + jnp.concatenate([kr] * MLA_HEADS, axis=1)).astype(BF16)
    v = jnp.dot(kvn, wv_ref[...], preferred_element_type=F32)
    lane = lax.broadcasted_iota(jnp.int32, v.shape, 1)
    v_ref[...] = jnp.where(lane % HEAD_PAD == MLA_DV, 1.0, v).astype(BF16)


def _mla_prep(cm, q_norm, kv_norm, wq, wk, wv, tabs_q, tabs_k, pos_block):
    R = cm.shape[0]
    tm = ROW_TILE
    W = MLA_HEADS * HEAD_PAD
    row = lambda w: pl.BlockSpec((tm, w), lambda i: (i, 0))
    full = lambda a: pl.BlockSpec(a.shape, lambda i: (0, 0))
    tab = lambda w: pl.BlockSpec((tm, w), lambda i: (pos_block(i), 0))
    outs = pl.pallas_call(
        _mla_prep_kernel,
        grid=(R // tm,),
        in_specs=[row(MLA_COLS_PAD), full(q_norm), full(kv_norm), full(wq), full(wk), full(wv),
                  tab(W), tab(W), tab(LANES), tab(LANES)],
        out_specs=[row(W)] * 3,
        out_shape=[jax.ShapeDtypeStruct((R, W), BF16)] * 3,
        compiler_params=_params("parallel"),
        name="mla_prep",
    )(cm, q_norm, kv_norm, wq, wk, wv, tabs_q[0], tabs_q[1], tabs_k[0], tabs_k[1])
    return outs


def _diff_prep_kernel(cd_ref, cos_ref, sin_ref, q_ref, k_ref, v_ref):
    w = DIFF_HEADS * 2 * DIFF_D
    cd = cd_ref[...].astype(F32)
    cos = cos_ref[...]
    sin = sin_ref[...]
    q = _rope(cd[:, :w], cos, sin, DIFF_D // 4, 0)
    q_ref[...] = (q * (DIFF_D ** -0.5 * LOG2E)).astype(BF16)
    k_ref[...] = _rope(cd[:, w:2 * w], cos, sin, DIFF_D // 4, 0).astype(BF16)
    v = cd_ref[:, 2 * w:]
    lane = lax.broadcasted_iota(jnp.int32, (v.shape[0], LANES), 1)
    ones = jnp.where(lane == 0, 1.0, 0.0).astype(BF16)
    parts = []
    for h in range(DIFF_HEADS):
        parts += [v[:, h * 2 * DIFF_D:(h + 1) * 2 * DIFF_D], ones]
    v_ref[...] = jnp.concatenate(parts, axis=1)


def _diff_prep(cd, tabs, pos_block):
    R = cd.shape[0]
    tm = ROW_TILE
    w = DIFF_HEADS * 2 * DIFF_D
    row = lambda c: pl.BlockSpec((tm, c), lambda i: (i, 0))
    tab = pl.BlockSpec((tm, w), lambda i: (pos_block(i), 0))
    return pl.pallas_call(
        _diff_prep_kernel,
        grid=(R // tm,),
        in_specs=[row(DIFF_COLS), tab, tab],
        out_specs=[row(w), row(w), row(2 * w)],
        out_shape=[jax.ShapeDtypeStruct((R, w), BF16), jax.ShapeDtypeStruct((R, w), BF16),
                   jax.ShapeDtypeStruct((R, 2 * w), BF16)],
        compiler_params=_params("parallel"),
        name="diff_prep",
    )(cd, tabs[0], tabs[1])


def _flash(streams, sources, vw, unroll):
    tq = streams[0][0].shape[0]
    carry = tuple((jnp.full((tq, 1), -jnp.inf, F32), jnp.zeros((tq, vw), F32)) for _ in streams)
    for k_ref, v_ref, length, tk in sources:
        def step(c, carry, k_ref=k_ref, v_ref=v_ref, tk=tk):
            r0 = pl.multiple_of(c * tk, tk)
            out = []
            for (q, kl0, vl0), (m, acc) in zip(streams, carry):
                k = k_ref[pl.ds(r0, tk), kl0:kl0 + HEAD_PAD]
                v = v_ref[pl.ds(r0, tk), vl0:vl0 + vw]
                s = lax.dot_general(q, k, (((1,), (1,)), ((), ())), preferred_element_type=F32)
                m_new = jnp.maximum(m, jnp.max(s, axis=1, keepdims=True))
                p = jnp.exp2(s - m_new)
                a = jnp.exp2(m - m_new)
                acc = a * acc + jnp.dot(p.astype(BF16), v, preferred_element_type=F32)
                out.append((m_new, acc))
            return tuple(out)
        n = length // tk
        if n == 1:
            carry = step(0, carry)
        else:
            carry = lax.fori_loop(0, n, step, carry, unroll=unroll)
    return [acc for _, acc in carry]


def _mla_attn_kernel(*refs, n_src, lens, tk, unroll):
    q_ref = refs[0]
    kv = refs[1:1 + 2 * n_src]
    o_ref = refs[1 + 2 * n_src]
    streams = [(q_ref[:, hh * HEAD_PAD:(hh + 1) * HEAD_PAD], hh * HEAD_PAD, hh * HEAD_PAD) for hh in range(2)]
    sources = [(kv[2 * s], kv[2 * s + 1], lens[s], min(tk, lens[s])) for s in range(n_src)]
    accs = _flash(streams, sources, HEAD_PAD, unroll)
    outs = [acc[:, :MLA_DV] / acc[:, MLA_DV:MLA_DV + 1] for acc in accs]
    o_ref[...] = jnp.concatenate(outs, axis=1).astype(BF16)


def _diff_attn_kernel(*refs, n_src, lens, tk, unroll, lam_init):
    q_ref, ll_ref, sub_ref = refs[:3]
    kv = refs[3:3 + 2 * n_src]
    o_ref = refs[3 + 2 * n_src]
    ll = ll_ref[...]
    lam = (jnp.exp(jnp.sum(ll[0:1, :] * ll[1:2, :], axis=1, keepdims=True))
           - jnp.exp(jnp.sum(ll[2:3, :] * ll[3:4, :], axis=1, keepdims=True)) + lam_init)
    q = q_ref[...]
    lane = lax.broadcasted_iota(jnp.int32, q.shape, 1)
    vw = 2 * HEAD_PAD
    dv = 2 * DIFF_D
    streams = [(jnp.where((lane // DIFF_D) == mp, q, jnp.zeros_like(q)), 0, 0) for mp in range(2)]
    sources = [(kv[2 * s], kv[2 * s + 1], lens[s], min(tk, lens[s])) for s in range(n_src)]
    accs = _flash(streams, sources, vw, unroll)
    o = [acc[:, :dv] / acc[:, dv:dv + 1] for acc in accs]
    od = o[0] - lam * o[1]
    o_ref[...] = (_rms(od, sub_ref[...]) * (1.0 - lam_init)).astype(BF16)


def _attention(kind, q, k, v, *, B, S, C, ctx_queries, extra=None, lam_init=None):
    N = B * S
    tk = ATTN_TK[kind]
    unroll = max(S // tk, 1)
    if kind == "mla":
        steps, qw, kw, vw, ow = MLA_HEADS // 2, 2 * HEAD_PAD, 2 * HEAD_PAD, 2 * HEAD_PAD, 2 * MLA_DV
    else:
        steps, qw, kw, vw, ow = DIFF_HEADS, HEAD_PAD, HEAD_PAD, 2 * HEAD_PAD, 2 * DIFF_D
    cblk = N // C
    if ctx_queries:
        tq, nq, rows_out = C, 1, B * C
        q_spec = pl.BlockSpec((tq, qw), lambda b, h, i: (cblk + b, h))
        kv_specs = [pl.BlockSpec((C, kw), lambda b, h, i: (cblk + b, h)),
                    pl.BlockSpec((C, vw), lambda b, h, i: (cblk + b, h))]
        lens = (C,)
    else:
        tq = ATTN_TQ
        nq, rows_out = S // tq, N
        q_spec = pl.BlockSpec((tq, qw), lambda b, h, i: (b * nq + i, h))
        kv_specs = [pl.BlockSpec((S, kw), lambda b, h, i: (b, h)),
                    pl.BlockSpec((S, vw), lambda b, h, i: (b, h)),
                    pl.BlockSpec((C, kw), lambda b, h, i: (cblk + b, h)),
                    pl.BlockSpec((C, vw), lambda b, h, i: (cblk + b, h))]
        lens = (S, C)
    n_src = len(lens)
    kv_args = [k, v] * n_src
    o_spec = pl.BlockSpec((tq, ow), lambda b, h, i: (b * nq + i, h))
    if kind == "mla":
        body = functools.partial(_mla_attn_kernel, n_src=n_src, lens=lens, tk=tk, unroll=unroll)
        args, specs = [q], [q_spec]
    else:
        ll, sub = extra
        body = functools.partial(_diff_attn_kernel, n_src=n_src, lens=lens, tk=tk, unroll=unroll, lam_init=lam_init)
        args = [q, ll, sub]
        specs = [q_spec, pl.BlockSpec(ll.shape, lambda b, h, i: (0, 0)), pl.BlockSpec(sub.shape, lambda b, h, i: (0, 0))]
    return pl.pallas_call(
        body,
        grid=(B, steps, nq),
        in_specs=specs + kv_specs,
        out_specs=o_spec,
        out_shape=jax.ShapeDtypeStruct((rows_out, steps * ow), BF16),
        compiler_params=_params("parallel", "parallel", "parallel"),
        name=f"{kind}_attn_{'ctx' if ctx_queries else 'lat'}",
    )(*args, *kv_args)


def _conv_kernel(main_ref, prev_ref, next_ref, w_ref, b_ref, g_ref, bb_ref, o_ref, z_ref, sh_ref, *, ts, nt):
    i = pl.program_id(1)

    def glu(blk):
        a = blk[:, :CONV_W].astype(F32)
        g = blk[:, CONV_W:].astype(F32)
        return a * jax.nn.sigmoid(g)

    H = CONV_HALO
    z_ref[0:H, :] = glu(prev_ref[...]) * (i > 0).astype(F32)
    z_ref[H:H + ts, :] = glu(main_ref[...])
    z_ref[H + ts:2 * H + ts, :] = glu(next_ref[...]) * (i < nt - 1).astype(F32)
    span = sh_ref.shape[1]
    for s in range(1, SUBLANES):
        sh_ref[s] = z_ref[s:s + span, :]
    rc = 32
    first = H - CONV_K // 2
    for r in range(ts // rc):
        acc = jnp.broadcast_to(b_ref[...], (rc, CONV_W))
        for k in range(CONV_K):
            s, a = (k + first) % SUBLANES, (k + first) // SUBLANES * SUBLANES
            src = z_ref[r * rc + a:r * rc + a + rc, :] if s == 0 else sh_ref[s, r * rc + a:r * rc + a + rc, :]
            acc = acc + w_ref[k:k + 1, :] * src
        mu = jnp.mean(acc, axis=-1, keepdims=True)
        d = acc - mu
        var = jnp.mean(d * d, axis=-1, keepdims=True)
        y = d * lax.rsqrt(var + EPS) * g_ref[...] + bb_ref[...]
        o_ref[r * rc:(r + 1) * rc, :] = (y * jax.nn.sigmoid(y)).astype(BF16)


def _conv(cols, row_off, nb, L, conv_w, conv_b, ln_g, ln_b):
    R = cols.shape[0]
    ts = min(ROW_TILE, L)
    nt = L // ts
    H = CONV_HALO
    hb = ts // H
    base = lambda b, i: (row_off + b * L + i * ts) // H
    vec = lambda a: pl.BlockSpec(a.shape, lambda b, i: (0, 0))
    cb, g, bb = conv_b.reshape(1, CONV_W), ln_g.reshape(1, CONV_W), ln_b.reshape(1, CONV_W)
    return pl.pallas_call(
        functools.partial(_conv_kernel, ts=ts, nt=nt),
        grid=(nb, nt),
        in_specs=[pl.BlockSpec((ts, CONV_COLS), lambda b, i: ((row_off + b * L) // ts + i, 0)),
                  pl.BlockSpec((H, CONV_COLS), lambda b, i: (jnp.maximum(base(b, i) - 1, 0), 0)),
                  pl.BlockSpec((H, CONV_COLS), lambda b, i: (jnp.minimum(base(b, i) + hb, R // H - 1), 0)),
                  vec(conv_w), vec(cb), vec(g), vec(bb)],
        out_specs=pl.BlockSpec((ts, CONV_W), lambda b, i: (b * nt + i, 0)),
        out_shape=jax.ShapeDtypeStruct((nb * L, CONV_W), BF16),
        scratch_shapes=[pltpu.VMEM((ts + 2 * H, CONV_W), F32),
                        pltpu.VMEM((SUBLANES, ts + 2 * H - SUBLANES, CONV_W), F32)],
        compiler_params=_params("parallel", "parallel"),
        name=f"conv_{L}",
    )(cols, cols, cols, conv_w, cb, g, bb)


def _dft_channel_matrix():
    j = jnp.arange(FNET_GROUP_W, dtype=jnp.int32)
    ang = (2.0 * math.pi / FNET_GROUP_W) * ((j[:, None] * j[None, :]) % FNET_GROUP_W).astype(F32)
    eye = jnp.eye(FNET_GROUPS, dtype=F32)
    scale = FNET_GROUP_W ** -0.5
    return jnp.concatenate([jnp.kron(eye, jnp.cos(ang)), jnp.kron(eye, jnp.sin(ang))], axis=1) * scale


def _dft_position_matrix(L):
    n = jnp.arange(L, dtype=jnp.int32)
    ang = (2.0 * math.pi / L) * ((n[:, None] * n[None, :]) % L).astype(F32)
    return jnp.concatenate([jnp.cos(ang), -jnp.sin(ang)], axis=1) * (L ** -0.5)


def _fnet_positions(y, a, row_off, nb, L):
    tm = min(ROW_TILE, L)
    tk = min(2048, L)
    kb = L // tk
    off = row_off // tk
    return _matmul(a, y, out_shape=(nb * L, FNET_COLS), out_dtype=F32, tm=tm, tn=FNET_COLS, tk=tk,
                   grid=(nb, L // tm, 1, 2 * kb),
                   a_map=lambda b, i, u, k: (i, k),
                   w_map=lambda b, i, u, k: (off + b * kb + k % kb, k // kb),
                   o_map=lambda b, i, u, k: (b * (L // tm) + i, 0),
                   name=f"fnet_pos_{L}")


FFT_MINOR = 128
FFT_GROUP = 8
FFT_COLS = 8


def _fft_major_matrix(n1):
    k = jnp.arange(n1, dtype=jnp.int32)
    ang = (2.0 * math.pi / n1) * ((k[:, None] * k[None, :]) % n1).astype(F32)
    return jnp.concatenate([jnp.cos(ang), jnp.sin(ang)], axis=0).astype(BF16)


def _fft_minor_matrices(L):
    n2c, grp = FFT_MINOR, FFT_GROUP
    n1 = L // n2c
    g = jnp.arange(n1 // grp, dtype=jnp.int32)[:, None, None, None, None]
    k2 = jnp.arange(n2c, dtype=jnp.int32)[None, :, None, None, None]
    j = jnp.arange(grp, dtype=jnp.int32)[None, None, :, None, None]
    jp = jnp.arange(grp, dtype=jnp.int32)[None, None, None, :, None]
    n2 = jnp.arange(n2c, dtype=jnp.int32)[None, None, None, None, :]
    phase = (n2 * k2 * n1 + n2 * (grp * g + j)) % L
    ang = (2.0 * math.pi / L) * phase.astype(F32)
    keep = (j == jp).astype(F32) * (L ** -0.5)
    shape = (n1 // grp, n2c * grp, grp * n2c)
    return ((jnp.cos(ang) * keep).reshape(shape).astype(BF16), (jnp.sin(ang) * keep).reshape(shape).astype(BF16))


def _fft_major_kernel(cs_ref, y_ref, a_ref, *, n1):
    pq = jnp.dot(cs_ref[...], y_ref[...], preferred_element_type=F32)
    p, q = pq[:n1], pq[n1:]
    w = FNET_COLS
    parts = []
    for t in range(FFT_COLS):
        re, im = slice(2 * t * w, (2 * t + 1) * w), slice((2 * t + 1) * w, (2 * t + 2) * w)
        parts += [p[:, re] - q[:, im], -(p[:, im] + q[:, re])]
    a_ref[...] = jnp.concatenate(parts, axis=1).astype(BF16)


def _fft_minor_kernel(mc_ref, ms_ref, a_ref, o_ref):
    w = FNET_COLS
    out = (jnp.dot(mc_ref[...], a_ref[:, :w], preferred_element_type=F32)
           + jnp.dot(ms_ref[...], a_ref[:, w:], preferred_element_type=F32))
    o_ref[...] = out.reshape(o_ref.shape)


def _fnet_positions_fft(y, nb, L):
    n2c, grp = FFT_MINOR, FFT_GROUP
    n1 = L // n2c
    w2 = 2 * FNET_COLS
    cs = _fft_major_matrix(n1)
    mc, ms = _fft_minor_matrices(L)
    flat = y[:nb * L].reshape(nb * n1, n2c * w2)
    tc = FFT_COLS * w2
    a = pl.pallas_call(
        functools.partial(_fft_major_kernel, n1=n1),
        grid=(nb, n2c // FFT_COLS),
        in_specs=[pl.BlockSpec((2 * n1, n1), lambda b, t: (0, 0)), pl.BlockSpec((n1, tc), lambda b, t: (b, t))],
        out_specs=pl.BlockSpec((n1, tc), lambda b, t: (b, t)),
        out_shape=jax.ShapeDtypeStruct((nb * n1, n2c * w2), BF16),
        compiler_params=_params("arbitrary", "arbitrary"),
        name="fft_major",
    )(cs, flat)
    a = a.reshape(nb * L, w2)
    rows = grp * n2c
    ng = n1 // grp
    out = pl.pallas_call(
        _fft_minor_kernel,
        grid=(ng, nb),
        in_specs=[pl.BlockSpec((None, rows, rows), lambda g, b: (g, 0, 0)),
                  pl.BlockSpec((None, rows, rows), lambda g, b: (g, 0, 0)),
                  pl.BlockSpec((rows, w2), lambda g, b: (b * ng + g, 0))],
        out_specs=pl.BlockSpec((None, n2c, grp, FNET_COLS), lambda g, b: (b, 0, g, 0)),
        out_shape=jax.ShapeDtypeStruct((nb, n2c, n1, FNET_COLS), F32),
        compiler_params=_params("arbitrary", "arbitrary"),
        name="fft_minor",
    )(mc, ms, a)
    return out.reshape(nb * L, FNET_COLS)


def _merge_kernel(*refs, lat_tiles):
    b_refs = refs[:2 * N_BRANCH]
    g_ref, wb_ref, wo_ref, o_ref = refs[2 * N_BRANCH:]
    D = D_MODEL
    is_ctx = pl.program_id(0) >= lat_tiles
    m = None
    for n in range(N_BRANCH):
        lat_ref, ctx_ref = b_refs[2 * n], b_refs[2 * n + 1]
        b = jnp.where(is_ctx, ctx_ref[...].astype(BF16), lat_ref[...].astype(BF16))
        proj = jnp.dot(b, wb_ref[n], preferred_element_type=F32)
        t = jax.nn.sigmoid(g_ref[:, n * D:(n + 1) * D].astype(F32)) * proj
        m = t if m is None else m + t
    o_ref[...] = jnp.dot(m.astype(BF16), wo_ref[...], preferred_element_type=F32)


def _merge(branches, gates, w_branch, w_out):
    R = gates.shape[0]
    D = D_MODEL
    tm = MERGE_TILE
    lat_tiles = branches[0][0].shape[0] // tm
    ctx_tiles = branches[0][1].shape[0] // tm
    lat_spec = pl.BlockSpec((tm, BRANCH_W), lambda i: (jnp.minimum(i, lat_tiles - 1), 0))
    ctx_spec = pl.BlockSpec((tm, BRANCH_W), lambda i: (jnp.clip(i - lat_tiles, 0, ctx_tiles - 1), 0))
    specs = [lat_spec, ctx_spec] * N_BRANCH
    branches = [a for pair in branches for a in pair]
    specs += [pl.BlockSpec((tm, N_BRANCH * D), lambda i: (i, 0)),
              pl.BlockSpec((N_BRANCH, BRANCH_W, D), lambda i: (0, 0, 0)),
              pl.BlockSpec((D, D), lambda i: (0, 0))]
    return pl.pallas_call(
        functools.partial(_merge_kernel, lat_tiles=lat_tiles),
        grid=(R // tm,),
        in_specs=specs,
        out_specs=pl.BlockSpec((tm, D), lambda i: (i, 0)),
        out_shape=jax.ShapeDtypeStruct((R, D), F32),
        compiler_params=_params("arbitrary"),
        name="merge",
    )(*branches, gates, w_branch, w_out)


def _moe_plan(rec, counts, R):
    T = MOE_TILE
    cnt = counts[:, 0].astype(jnp.int32)
    padded = ((cnt + T - 1) // T) * T
    ends = jnp.cumsum(padded)
    base = ends - padded
    n_tiles = (2 * R) // T + N_EXPERTS
    n_used = (ends[-1] // T).astype(jnp.int32)
    tile = jnp.minimum(jnp.arange(n_tiles, dtype=jnp.int32), n_used - 1)
    tile_expert = jnp.sum((ends // T)[None, :] <= tile[:, None], axis=1).astype(jnp.int32)
    tile_expert = jnp.minimum(tile_expert, N_EXPERTS - 1)
    e_ab = rec[0:2].astype(jnp.int32)
    r_ab = rec[2:4].astype(jnp.int32)
    base_ab = jnp.zeros_like(e_ab)
    for e in range(N_EXPERTS):
        base_ab = jnp.where(e_ab == e, base[e], base_ab)
    pos = (base_ab + r_ab - 1).T.reshape(2 * R)
    wgt = rec[4:6].T
    tail = jnp.concatenate([ends - T, (cnt > 0).astype(jnp.int32), n_used.reshape(1)]).astype(jnp.int32)
    return pos, wgt, tile_expert, n_used.reshape(1), tail


def _row_copy(src, src_row, dst, dst_row, sem):
    return pltpu.make_async_copy(src.at[pl.ds(src_row, 1)], dst.at[pl.ds(dst_row, 1)], sem)


def _dispatch_kernel(pos_ref, tail_ref, h_ref, buf_ref, zero_ref, sem, zsem, *, tm, n_tiles):
    i = pl.program_id(0)

    @pl.when(i == 0)
    def _():
        zero_ref[...] = jnp.zeros_like(zero_ref)

        def tail_copy(e):
            row = pl.multiple_of(tail_ref[e], MOE_TILE)
            return pltpu.make_async_copy(zero_ref, buf_ref.at[pl.ds(row, MOE_TILE)], zsem)

        for e in range(N_EXPERTS):
            @pl.when(tail_ref[N_EXPERTS + e] > 0)
            def _():
                tail_copy(e).start()
        for e in range(N_EXPERTS):
            @pl.when(tail_ref[N_EXPERTS + e] > 0)
            def _():
                tail_copy(e).wait()

        def unused_copy(t):
            return pltpu.make_async_copy(zero_ref, buf_ref.at[pl.ds(pl.multiple_of(t * MOE_TILE, MOE_TILE), MOE_TILE)],
                                         zsem)

        def clear(t, carry):
            unused_copy(t).start()
            unused_copy(t).wait()
            return carry

        lax.fori_loop(tail_ref[2 * N_EXPERTS], n_tiles, clear, 0)

    for t in range(tm):
        for slot in range(2):
            _row_copy(h_ref, t, buf_ref, pos_ref[(i * tm + t) * 2 + slot], sem).start()
    for slot in range(2):
        pltpu.make_async_copy(h_ref, buf_ref.at[pl.ds(0, tm)], sem).wait()


def _moe_dispatch(h, pos, tail, n_rows):
    R, D = h.shape
    tm = ROW_TILE
    return pl.pallas_call(
        functools.partial(_dispatch_kernel, tm=tm, n_tiles=n_rows // MOE_TILE),
        grid_spec=pltpu.PrefetchScalarGridSpec(
            num_scalar_prefetch=2,
            grid=(R // tm,),
            in_specs=[pl.BlockSpec((tm, D), lambda i, pos, tail: (i, 0))],
            out_specs=pl.BlockSpec(memory_space=pl.ANY),
            scratch_shapes=[pltpu.VMEM((MOE_TILE, D), F32), pltpu.SemaphoreType.DMA, pltpu.SemaphoreType.DMA],
        ),
        out_shape=jax.ShapeDtypeStruct((n_rows, D), F32),
        compiler_params=_params("arbitrary"),
        name="moe_dispatch",
    )(pos, tail, h)


def _experts_kernel(te_ref, nu_ref, x_ref, w1_ref, w3_ref, w2_ref, o_ref):
    @pl.when(pl.program_id(0) >= nu_ref[0])
    def _():
        o_ref[...] = jnp.zeros_like(o_ref)

    @pl.when(pl.program_id(0) < nu_ref[0])
    def _():
        x = x_ref[...].astype(BF16)
        a = jnp.dot(x, w1_ref[...], preferred_element_type=F32)
        b = jnp.dot(x, w3_ref[...], preferred_element_type=F32)
        hid = (a * jax.nn.sigmoid(a) * b).astype(BF16)
        o_ref[...] = jnp.dot(hid, w2_ref[...], preferred_element_type=F32)


def _moe_experts(xs, tile_expert, n_used, w1, w3, w2):
    P, D = xs.shape
    T = MOE_TILE
    row = pl.BlockSpec((T, D), lambda i, te, nu: (i, 0))
    return pl.pallas_call(
        _experts_kernel,
        grid_spec=pltpu.PrefetchScalarGridSpec(
            num_scalar_prefetch=2,
            grid=(P // T,),
            in_specs=[row,
                      pl.BlockSpec((None, D, D_FF), lambda i, te, nu: (te[i], 0, 0)),
                      pl.BlockSpec((None, D, D_FF), lambda i, te, nu: (te[i], 0, 0)),
                      pl.BlockSpec((None, D_FF, D), lambda i, te, nu: (te[i], 0, 0))],
            out_specs=row,
        ),
        out_shape=jax.ShapeDtypeStruct((P, D), F32),
        compiler_params=_params("arbitrary"),
        name="moe_experts",
    )(tile_expert, n_used, xs, w1, w3, w2)


def _combine_norm_kernel(*refs, tc, n_tiles, alpha, emit_h):
    refs = list(refs)
    pos_ref, x_ref, w_ref, g_ref, lng_ref, lnb_ref = refs[:6]
    refs = refs[6:]
    if emit_h:
        sc_ref, sh_ref = refs[:2]
        refs = refs[2:]
    y_ref, xo_ref = refs[:2]
    refs = refs[2:]
    if emit_h:
        h_ref = refs.pop(0)
    bufs = (refs[0:2], refs[2:4])
    sem = refs[4]
    i = pl.program_id(0)

    def issue(tile, slot, t):
        p = (tile * tc + t) * 2
        _row_copy(y_ref, pos_ref[p], bufs[slot][0], t, sem.at[slot]).start()
        _row_copy(y_ref, pos_ref[p + 1], bufs[slot][1], t, sem.at[slot]).start()

    def wait(slot):
        for buf in bufs[slot]:
            pltpu.make_async_copy(y_ref.at[pl.ds(0, tc)], buf, sem.at[slot]).wait()

    @pl.when(i == 0)
    def _():
        lax.fori_loop(0, tc, lambda t, c: (issue(0, 0, t), c)[1], 0, unroll=8)

    nxt = jnp.minimum(i + 1, n_tiles - 1)
    for slot in range(2):
        @pl.when(i % 2 == slot)
        def _(slot=slot):
            wait(slot)
            for t in range(tc):
                issue(nxt, 1 - slot, t)
            ya_ref, yb_ref = bufs[slot]
            w = w_ref[...]
            f = w[:, 0:1] * ya_ref[...] + w[:, 1:2] * yb_ref[...]
            xn = _standardize(alpha * x_ref[...] + g_ref[...] * f) * lng_ref[...] + lnb_ref[...]
            xo_ref[...] = xn
            if emit_h:
                h_ref[...] = (xn * (1.0 + sc_ref[...]) + sh_ref[...]).astype(BF16)

            @pl.when(i == n_tiles - 1)
            def _():
                wait(1 - slot)


def _moe_combine_norm(x, ys, pos, wgt, *, seg_of_row, alpha, mod_g, k_gate, ln_g, ln_b, mod_h=None, k_scale=None,
                      k_shift=None, name):
    R, D = x.shape
    tc = COMBINE_TILE
    emit_h = mod_h is not None
    row = pl.BlockSpec((tc, D), lambda i, pos: (i, 0))

    def mod_spec(k):
        return pl.BlockSpec((None, 1, D), lambda i, pos: (seg_of_row(i * tc) * 6 + k, 0, 0))

    vec = pl.BlockSpec((1, D), lambda i, pos: (0, 0))
    args = [x, wgt, mod_g, ln_g.reshape(1, D), ln_b.reshape(1, D)]
    specs = [row, pl.BlockSpec((tc, 2), lambda i, pos: (i, 0)), mod_spec(k_gate), vec, vec]
    if emit_h:
        args += [mod_h, mod_h]
        specs += [mod_spec(k_scale), mod_spec(k_shift)]
    args.append(ys)
    specs.append(pl.BlockSpec(memory_space=pl.ANY))
    out_shape = [jax.ShapeDtypeStruct((R, D), F32)]
    out_specs = [row]
    if emit_h:
        out_shape.append(jax.ShapeDtypeStruct((R, D), BF16))
        out_specs.append(row)
    return pl.pallas_call(
        functools.partial(_combine_norm_kernel, tc=tc, n_tiles=R // tc, alpha=alpha, emit_h=emit_h),
        grid_spec=pltpu.PrefetchScalarGridSpec(
            num_scalar_prefetch=1,
            grid=(R // tc,),
            in_specs=specs,
            out_specs=out_specs,
            scratch_shapes=[pltpu.VMEM((tc, D), F32)] * 4 + [pltpu.SemaphoreType.DMA((2,))],
        ),
        out_shape=out_shape,
        compiler_params=_params("arbitrary"),
        name=name,
    )(pos, *args)


def _pad_heads(w, heads, width, lo, hi):
    lead = w.shape[:-1]
    w = w.reshape(lead + (heads, width))[..., lo:hi]
    w = jnp.pad(w, ((0, 0),) * (len(lead) + 1) + ((0, HEAD_PAD - (hi - lo)),))
    return w.reshape(lead + (heads * HEAD_PAD,))


def kernel(x, c, ctx, c_ctx, w_ada, b_ada, w_in, mla_q_norm, mla_w_uq, mla_kv_norm, mla_w_ukv, diff_lq1, diff_lk1,
           diff_lq2, diff_lk2, diff_subln, conv_w, conv_b, conv_ln_g, conv_ln_b, w_branch, w_out, ln1_g, ln1_b,
           ln2_g, ln2_b, router_w, router_bias, exp_w1, exp_w3, exp_w2):
    B, S, D = x.shape
    C = ctx.shape[1]
    depth = w_in.shape[0]
    N, NC = B * S, B * C
    tm = ROW_TILE
    assert D == D_MODEL and S % tm == 0 and NC % tm == 0 and S % C == 0 and S % GRID_W == 0

    R = N + NC

    def seg_of_row(r):
        return jnp.minimum(r // S, B)

    def seg(i):
        return seg_of_row(i * tm)

    def pos_block(i):
        return jnp.where(i < N // tm, i % (S // tm), S // tm)

    rows = S // GRID_W
    tabs_mq = _rope_tables(rows, MLA_DR, MLA_DN, HEAD_PAD, MLA_HEADS, tm)
    tabs_mk = _rope_tables(rows, MLA_DR, 0, LANES, 1, tm)
    tabs_d = _rope_tables(rows, DIFF_D, 0, DIFF_D, 2 * DIFF_HEADS, tm)

    nseg = 8
    cond = jnp.zeros((nseg, D), F32).at[:B].set(c).at[B].set(c_ctx)
    mod_all = _ada(cond, w_ada, b_ada).reshape(depth, nseg * 6, 1, D)

    dft_ch = _dft_channel_matrix().astype(BF16)
    dft_ctx = _dft_position_matrix(C).astype(BF16)
    if S % (FFT_MINOR * FFT_GROUP) == 0:
        fnet_lat = lambda y: _fnet_positions_fft(y, B, S)
    else:
        dft_lat = _dft_position_matrix(S).astype(BF16)
        fnet_lat = lambda y: _fnet_positions(y, dft_lat, 0, B, S)
    rw_t = router_w.T
    rw_hi = rw_t.astype(BF16)
    rw_lo = (rw_t - rw_hi.astype(F32)).astype(BF16)
    router = (rw_hi, rw_lo, router_bias.reshape(N_EXPERTS, 1))

    xs = jnp.concatenate([x.reshape(N, D), ctx.reshape(NC, D)], axis=0)
    alpha = (2.0 * depth) ** 0.25
    xs, h = _norm(xs, seg=seg, alpha=alpha, first=True, mod_h=mod_all[0], k_scale=1, k_shift=0, name="norm_in")

    w_mla = jnp.pad(w_in[:, :, :OFF_DIFF], ((0, 0), (0, 0), (0, MLA_COLS_PAD - MLA_COLS))).astype(BF16)
    w_rest = w_in[:, :, OFF_DIFF:].astype(BF16)
    wq = _pad_heads(mla_w_uq, MLA_HEADS, MLA_DN + MLA_DR, 0, MLA_DN + MLA_DR).astype(BF16)
    wk = _pad_heads(mla_w_ukv, MLA_HEADS, MLA_DN + MLA_DV, 0, MLA_DN).astype(BF16)
    wv = _pad_heads(mla_w_ukv, MLA_HEADS, MLA_DN + MLA_DV, MLA_DN, MLA_DN + MLA_DV).astype(BF16)

    for l in range(depth):
        mod = mod_all[l]
        cm = _project(h, w_mla, F32, "proj_mla", MLA_COLS_PAD, layer=l)
        cd = _project(h, w_rest, BF16, "proj_diff", DIFF_COLS, layer=l, col0=0, n_cols=DIFF_COLS)
        cc = _project(h, w_rest, BF16, "proj_conv", CONV_W, layer=l, col0=OFF_CONV - OFF_DIFF, n_cols=CONV_COLS)
        cf = _project(h, w_rest, BF16, "proj_fnet", FNET_COLS, layer=l, col0=OFF_FNET - OFF_DIFF, n_cols=FNET_COLS)
        gates = _project(h, w_rest, BF16, "proj_gate", 1024, layer=l, col0=OFF_GATE - OFF_DIFF, n_cols=GATE_COLS,
                         tm=2 * tm if R % (2 * tm) == 0 else tm)

        mq, mk, mv = _mla_prep(cm, mla_q_norm[l].reshape(1, -1), mla_kv_norm[l].reshape(1, -1), wq[l], wk[l], wv[l],
                               tabs_mq, tabs_mk, pos_block)
        o_mla = (_attention("mla", mq, mk, mv, B=B, S=S, C=C, ctx_queries=False),
                 _attention("mla", mq, mk, mv, B=B, S=S, C=C, ctx_queries=True))

        lam_init = 0.8 - 0.6 * math.exp(-0.3 * l)
        ll = jnp.stack([diff_lq1[l], diff_lk1[l], diff_lq2[l], diff_lk2[l]], axis=0)
        sub = diff_subln[l].reshape(1, -1)
        dq, dk, dv = _diff_prep(cd, tabs_d, pos_block)
        o_diff = (_attention("diff", dq, dk, dv, B=B, S=S, C=C, ctx_queries=False, extra=(ll, sub), lam_init=lam_init),
                  _attention("diff", dq, dk, dv, B=B, S=S, C=C, ctx_queries=True, extra=(ll, sub), lam_init=lam_init))

        o_conv = (_conv(cc, 0, B, S, conv_w[l], conv_b[l], conv_ln_g[l], conv_ln_b[l]),
                  _conv(cc, N, B, C, conv_w[l], conv_b[l], conv_ln_g[l], conv_ln_b[l]))
        y = _project(cf, dft_ch, BF16, "fnet_ch", 2 * FNET_COLS)
        o_fnet = (fnet_lat(y), _fnet_positions(y, dft_ctx, N, B, C))

        mix = _merge([o_mla, o_diff, o_conv, o_fnet], gates, w_branch[l].astype(BF16), w_out[l].astype(BF16))
        xs, h2, rec, counts = _norm(xs, seg=seg, alpha=alpha, m=mix, mod_g=mod, k_gate=2, ln_g=ln1_g[l],
                                    ln_b=ln1_b[l], mod_h=mod, k_scale=4, k_shift=3, router=router, name="norm1")
        pos, wgt, tile_expert, n_used, tail = _moe_plan(rec, counts, R)
        sorted_rows = _moe_dispatch(h2, pos, tail, tile_expert.shape[0] * MOE_TILE)
        ys = _moe_experts(sorted_rows, tile_expert, n_used, exp_w1[l].astype(BF16), exp_w3[l].astype(BF16),
                          exp_w2[l].astype(BF16))
        last = l + 1 == depth
        out = _moe_combine_norm(xs, ys, pos, wgt, seg_of_row=seg_of_row, alpha=alpha, mod_g=mod, k_gate=5,
                                ln_g=ln2_g[l], ln_b=ln2_b[l], mod_h=None if last else mod_all[l + 1],
                                k_scale=1, k_shift=0, name="norm_out" if last else "norm2")
        if last:
            (xs,) = out
        else:
            xs, h = out
    return xs[:N].reshape(B, S, D)
```

```python
import functools
import math

import jax
import jax.numpy as jnp
from jax import lax
from jax.experimental import pallas as pl
from jax.experimental.pallas import tpu as pltpu

F32 = jnp.float32
BF16 = jnp.bfloat16

D_MODEL = 2048
GRID_W = 64
ROPE_THETA = 10000.0
EPS = 1e-6

MLA_HEADS = 8
MLA_DN = 64
MLA_DR = 32
MLA_DV = 64
MLA_Q_RANK = 384
MLA_KV_RANK = 256
DIFF_HEADS = 4
DIFF_D = 64
CONV_W = 512
CONV_K = 31
FNET_GROUPS = 4
FNET_GROUP_W = 128
N_BRANCH = 4
BRANCH_W = 512
N_EXPERTS = 16
N_GROUPS = 4
EXPERTS_PER_GROUP = N_EXPERTS // N_GROUPS
D_FF = 1024

MLA_COLS = MLA_Q_RANK + MLA_KV_RANK + MLA_DR
DIFF_COLS = 3 * DIFF_HEADS * 2 * DIFF_D
CONV_COLS = 2 * CONV_W
FNET_COLS = FNET_GROUPS * FNET_GROUP_W
GATE_COLS = N_BRANCH * D_MODEL
OFF_DIFF = MLA_COLS
OFF_CONV = OFF_DIFF + DIFF_COLS
OFF_FNET = OFF_CONV + CONV_COLS
OFF_GATE = OFF_FNET + FNET_COLS

LANES = 128
SUBLANES = 8
HEAD_PAD = 128
MLA_COLS_PAD = 768
CONV_HALO = 16
VMEM_LIMIT = 56 * 1024 * 1024
LOG2E = math.log2(math.e)

ROW_TILE = 512
MERGE_TILE = 256
MOE_TILE = 512
COMBINE_TILE = 256
ATTN_TQ = 512
ATTN_TK = {"mla": 2048, "diff": 1024}


def _params(*sem):
    return pltpu.CompilerParams(dimension_semantics=("arbitrary",) * len(sem), vmem_limit_bytes=VMEM_LIMIT)


def _mm_kernel(a_ref, w_ref, o_ref, acc_ref, *, nk):
    prod = jnp.dot(a_ref[...], w_ref[...], preferred_element_type=F32)
    if nk == 1:
        o_ref[...] = prod.astype(o_ref.dtype)
        return
    k = pl.program_id(3)

    @pl.when(k == 0)
    def _():
        acc_ref[...] = prod

    @pl.when(k > 0)
    def _():
        acc_ref[...] += prod

    @pl.when(k == nk - 1)
    def _():
        o_ref[...] = acc_ref[...].astype(o_ref.dtype)


def _matmul(a, w, *, out_shape, out_dtype, tm, tn, tk, grid, a_map, o_map, name, w_map=None, w_spec=None):
    nk = grid[3]
    if w_spec is None:
        w_spec = pl.BlockSpec((tk, tn), w_map)
    return pl.pallas_call(
        functools.partial(_mm_kernel, nk=nk),
        grid=grid,
        in_specs=[pl.BlockSpec((tm, tk), a_map), w_spec],
        out_specs=pl.BlockSpec((tm, tn), o_map),
        out_shape=jax.ShapeDtypeStruct(out_shape, out_dtype),
        scratch_shapes=[pltpu.VMEM((tm, tn), F32)],
        compiler_params=_params("parallel", "parallel", "parallel", "arbitrary"),
        name=name,
    )(a, w)


def _project(h, w, out_dtype, name, tn, layer=None):
    R, K = h.shape
    N = w.shape[-1]
    tm = ROW_TILE
    if layer is None:
        w_spec = pl.BlockSpec((K, tn), lambda i, j, u, k: (0, j))
    else:
        w_spec = pl.BlockSpec((None, K, tn), lambda i, j, u, k: (layer, 0, j))
    return _matmul(h, w, out_shape=(R, N), out_dtype=out_dtype, tm=tm, tn=tn, tk=K,
                   grid=(R // tm, N // tn, 1, 1),
                   a_map=lambda i, j, u, k: (i, 0), w_spec=w_spec,
                   o_map=lambda i, j, u, k: (i, j), name=name)


def _ada_kernel(c_ref, w_ref, b_ref, o_ref):
    c = c_ref[...]
    a = (c * jax.nn.sigmoid(c)).astype(BF16)
    o_ref[...] = jnp.dot(a, w_ref[...].astype(BF16), preferred_element_type=F32) + b_ref[...]


def _ada(cond, w_ada, b_ada):
    L, D, N6 = w_ada.shape
    rows = cond.shape[0]
    tn = 1024
    return pl.pallas_call(
        _ada_kernel,
        grid=(L, N6 // tn),
        in_specs=[pl.BlockSpec((rows, D), lambda l, j: (0, 0)),
                  pl.BlockSpec((None, D, tn), lambda l, j: (l, 0, j)),
                  pl.BlockSpec((None, 1, tn), lambda l, j: (l, 0, j))],
        out_specs=pl.BlockSpec((None, rows, tn), lambda l, j: (l, 0, j)),
        out_shape=jax.ShapeDtypeStruct((L, rows, N6), F32),
        compiler_params=_params("parallel", "parallel"),
        name="ada_mod",
    )(cond, w_ada, b_ada.reshape(L, 1, N6))


def _route(logits_t, bias_t):
    scores = jax.nn.sigmoid(logits_t)
    sel = scores + bias_t
    sel_r = [sel[e:e + 1, :] for e in range(N_EXPERTS)]
    sc_r = [scores[e:e + 1, :] for e in range(N_EXPERTS)]
    per = EXPERTS_PER_GROUP
    gscore = []
    for g in range(N_GROUPS):
        r = sel_r[g * per:(g + 1) * per]
        best = None
        for a in range(per):
            for b in range(a + 1, per):
                s = r[a] + r[b]
                best = s if best is None else jnp.maximum(best, s)
        gscore.append(best)
    best_g = jnp.zeros_like(gscore[0], dtype=jnp.int32)
    best_v = gscore[0]
    for g in range(1, N_GROUPS):
        better = gscore[g] > best_v
        best_g = jnp.where(better, g, best_g)
        best_v = jnp.where(better, gscore[g], best_v)
    v = []
    s = []
    for j in range(per):
        vj = sel_r[j]
        sj = sc_r[j]
        for g in range(1, N_GROUPS):
            vj = jnp.where(best_g == g, sel_r[g * per + j], vj)
            sj = jnp.where(best_g == g, sc_r[g * per + j], sj)
        v.append(vj)
        s.append(sj)
    chosen = []
    for j in range(per):
        rank = jnp.zeros_like(best_g)
        for i in range(per):
            if i == j:
                continue
            ahead = (v[i] > v[j]) | ((v[i] == v[j]) & (i < j))
            rank = rank + ahead.astype(jnp.int32)
        chosen.append(rank < 2)
    total = jnp.zeros_like(s[0])
    for j in range(per):
        total = total + jnp.where(chosen[j], s[j], 0.0)
    picked, weight = [], []
    for e in range(N_EXPERTS):
        g, j = divmod(e, per)
        on = (best_g == g) & chosen[j]
        picked.append(on)
        weight.append(jnp.where(on, s[j] / total, 0.0))
    return picked, weight


def _dispatch_info(picked, weight, tri_ref, tot_ref):
    sel = jnp.concatenate([p.astype(F32) for p in picked], axis=0)
    cum = jnp.dot(sel.astype(BF16), tri_ref[...], preferred_element_type=F32)
    tot = tot_ref[...]
    rank = cum + tot[:, 0:1]
    tot_ref[...] = tot + jnp.sum(sel, axis=1, keepdims=True)
    zero = jnp.zeros_like(weight[0])
    seen = zero
    rec = [zero] * 6
    for e in range(N_EXPERTS):
        on = picked[e]
        for slot, is_slot in ((0, on & (seen == 0.0)), (1, on & (seen == 1.0))):
            rec[slot] = jnp.where(is_slot, float(e), rec[slot])
            rec[2 + slot] = jnp.where(is_slot, rank[e:e + 1, :], rec[2 + slot])
            rec[4 + slot] = jnp.where(is_slot, weight[e], rec[4 + slot])
        seen = seen + on.astype(F32)
    return jnp.concatenate(rec + [zero, zero], axis=0)


def _standardize(y):
    mu = jnp.mean(y, axis=-1, keepdims=True)
    d = y - mu
    var = jnp.mean(d * d, axis=-1, keepdims=True)
    return d * lax.rsqrt(var + EPS)


def _norm_kernel(*refs, first, emit_h, router, alpha):
    refs = list(refs)
    x_ref = refs.pop(0)
    if not first:
        m_ref, g_ref, lng_ref, lnb_ref = refs[:4]
        refs = refs[4:]
    if emit_h:
        sc_ref, sh_ref = refs[:2]
        refs = refs[2:]
    if router:
        rwh_ref, rwl_ref, rb_ref, tri_ref = refs[:4]
        refs = refs[4:]
    xo_ref = refs.pop(0)
    x = x_ref[...]
    xn = _standardize(x if first else alpha * x + g_ref[...] * m_ref[...])
    if not first:
        xn = xn * lng_ref[...] + lnb_ref[...]
    xo_ref[...] = xn
    if not emit_h:
        return
    h_ref = refs.pop(0)
    h = xn * (1.0 + sc_ref[...]) + sh_ref[...]
    h_ref[...] = h.astype(h_ref.dtype)
    if router:
        rec_ref, cnt_ref, tot_ref = refs

        @pl.when(pl.program_id(0) == 0)
        def _():
            tot_ref[...] = jnp.zeros_like(tot_ref)

        h_hi = h.astype(BF16)
        h_lo = (h - h_hi.astype(F32)).astype(BF16)
        nt = (((1,), (1,)), ((), ()))
        logits_t = (lax.dot_general(rwh_ref[...], h_hi, nt, preferred_element_type=F32)
                    + lax.dot_general(rwh_ref[...], h_lo, nt, preferred_element_type=F32)
                    + lax.dot_general(rwl_ref[...], h_hi, nt, preferred_element_type=F32))
        picked, weight = _route(logits_t, rb_ref[...])
        rec_ref[...] = _dispatch_info(picked, weight, tri_ref, tot_ref)
        cnt_ref[...] = tot_ref[...]


def _norm(x, *, seg, alpha, first=False, m=None, mod_g=None, k_gate=None, ln_g=None, ln_b=None,
          mod_h=None, k_scale=None, k_shift=None, router=None, name):
    R, D = x.shape
    tm = ROW_TILE
    emit_h = mod_h is not None
    row = pl.BlockSpec((tm, D), lambda i: (i, 0))

    def mod_spec(k):
        return pl.BlockSpec((None, 1, D), lambda i: (seg(i) * 6 + k, 0, 0))

    vec = pl.BlockSpec((1, D), lambda i: (0, 0))
    args, specs = [x], [row]
    if not first:
        args += [m, mod_g, ln_g.reshape(1, D), ln_b.reshape(1, D)]
        specs += [row, mod_spec(k_gate), vec, vec]
    if emit_h:
        args += [mod_h, mod_h]
        specs += [mod_spec(k_scale), mod_spec(k_shift)]
    scratch = []
    if router is not None:
        rwh, rwl, rb = router
        tri = (jnp.arange(tm)[:, None] <= jnp.arange(tm)[None, :]).astype(BF16)
        args += [rwh, rwl, rb, tri]
        specs += [pl.BlockSpec((N_EXPERTS, D), lambda i: (0, 0))] * 2
        specs += [pl.BlockSpec((N_EXPERTS, 1), lambda i: (0, 0)), pl.BlockSpec((tm, tm), lambda i: (0, 0))]
        scratch = [pltpu.VMEM((N_EXPERTS, LANES), F32)]
    out_shape = [jax.ShapeDtypeStruct((R, D), F32)]
    out_specs = [row]
    if emit_h:
        out_shape.append(jax.ShapeDtypeStruct((R, D), BF16 if router is None else F32))
        out_specs.append(row)
    if router is not None:
        out_shape += [jax.ShapeDtypeStruct((8, R), F32), jax.ShapeDtypeStruct((N_EXPERTS, LANES), F32)]
        out_specs += [pl.BlockSpec((8, tm), lambda i: (0, i)), pl.BlockSpec((N_EXPERTS, LANES), lambda i: (0, 0))]
    return pl.pallas_call(
        functools.partial(_norm_kernel, first=first, emit_h=emit_h, router=router is not None, alpha=alpha),
        grid=(R // tm,),
        in_specs=specs,
        out_specs=out_specs,
        out_shape=out_shape,
        scratch_shapes=scratch,
        compiler_params=_params("arbitrary" if router is not None else "parallel"),
        name=name,
    )(*args)


def _rope_tables(rows, rot_dim, lane0, chunk, reps, ident_rows):
    n = rows * GRID_W
    r = jnp.broadcast_to(jnp.arange(rows, dtype=F32)[:, None], (rows, GRID_W)).reshape(n)
    c = jnp.broadcast_to(jnp.arange(GRID_W, dtype=F32)[None, :], (rows, GRID_W)).reshape(n)
    quarter = rot_dim // 4
    inv_freq = ROPE_THETA ** (-jnp.arange(quarter, dtype=F32) / quarter)
    ar = r[:, None] * inv_freq
    ac = c[:, None] * inv_freq
    ang = jnp.concatenate([ar, ar, ac, ac], -1)
    sign = jnp.concatenate([-jnp.ones((quarter,), F32), jnp.ones((quarter,), F32)] * 2)
    cos = jnp.ones((n, chunk), F32).at[:, lane0:lane0 + rot_dim].set(jnp.cos(ang))
    sin = jnp.zeros((n, chunk), F32).at[:, lane0:lane0 + rot_dim].set(jnp.sin(ang) * sign)
    cos = jnp.concatenate([cos, jnp.ones((ident_rows, chunk), F32)], 0)
    sin = jnp.concatenate([sin, jnp.zeros((ident_rows, chunk), F32)], 0)
    return jnp.tile(cos, (1, reps)), jnp.tile(sin, (1, reps))


def _rope(x, cos, sin_signed, quarter, lane0):
    w = x.shape[-1]
    lane = lax.broadcasted_iota(jnp.int32, x.shape, 1)
    even = (((lane - lane0) // quarter) % 2) == 0
    partner = jnp.where(even, pltpu.roll(x, w - quarter, 1), pltpu.roll(x, quarter, 1))
    return x * cos + partner * sin_signed


def _rms(x, g):
    return x * lax.rsqrt(jnp.mean(x * x, axis=-1, keepdims=True) + EPS) * g


def _mla_prep_kernel(cm_ref, qn_ref, kvn_ref, wq_ref, wk_ref, wv_ref, cq_ref, sq_ref, ck_ref, sk_ref,
                     q_ref, k_ref, v_ref):
    cm = cm_ref[...]
    cq = cm[:, :MLA_Q_RANK]
    ckv = cm[:, MLA_Q_RANK:MLA_Q_RANK + MLA_KV_RANK]
    kr = cm[:, MLA_Q_RANK + MLA_KV_RANK:]
    qn = _rms(cq, qn_ref[...]).astype(BF16)
    kvn = _rms(ckv, kvn_ref[...]).astype(BF16)
    q = jnp.dot(qn, wq_ref[...], preferred_element_type=F32)
    q = _rope(q, cq_ref[...], sq_ref[...], MLA_DR // 4, MLA_DN)
    scale = (MLA_DN + MLA_DR) ** -0.5 * LOG2E
    q_ref[...] = (q * scale).astype(BF16)
    kr = _rope(kr, ck_ref[...], sk_ref[...], MLA_DR // 4, 0)
    kr = pltpu.roll(kr, MLA_DN, 1)
    k = jnp.dot(kvn, wk_ref[...], preferred_element_type=F32)
    k_ref[...] = (k + jnp.concatenate([kr] * MLA_HEADS, axis=1)).astype(BF16)
    v = jnp.dot(kvn, wv_ref[...], preferred_element_type=F32)
    lane = lax.broadcasted_iota(jnp.int32, v.shape, 1)
    v_ref[...] = jnp.where(lane % HEAD_PAD == MLA_DV, 1.0, v).astype(BF16)


def _mla_prep(cm, q_norm, kv_norm, wq, wk, wv, tabs_q, tabs_k, pos_block):
    R = cm.shape[0]
    tm = ROW_TILE
    W = MLA_HEADS * HEAD_PAD
    row = lambda w: pl.BlockSpec((tm, w), lambda i: (i, 0))
    full = lambda a: pl.BlockSpec(a.shape, lambda i: (0, 0))
    tab = lambda w: pl.BlockSpec((tm, w), lambda i: (pos_block(i), 0))
    outs = pl.pallas_call(
        _mla_prep_kernel,
        grid=(R // tm,),
        in_specs=[row(MLA_COLS_PAD), full(q_norm), full(kv_norm), full(wq), full(wk), full(wv),
                  tab(W), tab(W), tab(LANES), tab(LANES)],
        out_specs=[row(W)] * 3,
        out_shape=[jax.ShapeDtypeStruct((R, W), BF16)] * 3,
        compiler_params=_params("parallel"),
        name="mla_prep",
    )(cm, q_norm, kv_norm, wq, wk, wv, tabs_q[0], tabs_q[1], tabs_k[0], tabs_k[1])
    return outs


def _diff_prep_kernel(cd_ref, cos_ref, sin_ref, q_ref, k_ref, v_ref):
    w = DIFF_HEADS * 2 * DIFF_D
    cd = cd_ref[...].astype(F32)
    cos = cos_ref[...]
    sin = sin_ref[...]
    q = _rope(cd[:, :w], cos, sin, DIFF_D // 4, 0)
    q_ref[...] = (q * (DIFF_D ** -0.5 * LOG2E)).astype(BF16)
    k_ref[...] = _rope(cd[:, w:2 * w], cos, sin, DIFF_D // 4, 0).astype(BF16)
    v = cd_ref[:, 2 * w:]
    lane = lax.broadcasted_iota(jnp.int32, (v.shape[0], LANES), 1)
    ones = jnp.where(lane == 0, 1.0, 0.0).astype(BF16)
    parts = []
    for h in range(DIFF_HEADS):
        parts += [v[:, h * 2 * DIFF_D:(h + 1) * 2 * DIFF_D], ones]
    v_ref[...] = jnp.concatenate(parts, axis=1)


def _diff_prep(cd, tabs, pos_block):
    R = cd.shape[0]
    tm = ROW_TILE
    w = DIFF_HEADS * 2 * DIFF_D
    row = lambda c: pl.BlockSpec((tm, c), lambda i: (i, 0))
    tab = pl.BlockSpec((tm, w), lambda i: (pos_block(i), 0))
    return pl.pallas_call(
        _diff_prep_kernel,
        grid=(R // tm,),
        in_specs=[row(DIFF_COLS), tab, tab],
        out_specs=[row(w), row(w), row(2 * w)],
        out_shape=[jax.ShapeDtypeStruct((R, w), BF16), jax.ShapeDtypeStruct((R, w), BF16),
                   jax.ShapeDtypeStruct((R, 2 * w), BF16)],
        compiler_params=_params("parallel"),
        name="diff_prep",
    )(cd, tabs[0], tabs[1])


def _flash(streams, sources, vw, unroll):
    tq = streams[0][0].shape[0]
    carry = tuple((jnp.full((tq, 1), -jnp.inf, F32), jnp.zeros((tq, vw), F32)) for _ in streams)
    for k_ref, v_ref, length, tk in sources:
        def step(c, carry, k_ref=k_ref, v_ref=v_ref, tk=tk):
            r0 = pl.multiple_of(c * tk, tk)
            out = []
            for (q, kl0, vl0), (m, acc) in zip(streams, carry):
                k = k_ref[pl.ds(r0, tk), kl0:kl0 + HEAD_PAD]
                v = v_ref[pl.ds(r0, tk), vl0:vl0 + vw]
                s = lax.dot_general(q, k, (((1,), (1,)), ((), ())), preferred_element_type=F32)
                m_new = jnp.maximum(m, jnp.max(s, axis=1, keepdims=True))
                p = jnp.exp2(s - m_new)
                a = jnp.exp2(m - m_new)
                acc = a * acc + jnp.dot(p.astype(BF16), v, preferred_element_type=F32)
                out.append((m_new, acc))
            return tuple(out)
        n = length // tk
        if n == 1:
            carry = step(0, carry)
        else:
            carry = lax.fori_loop(0, n, step, carry, unroll=unroll)
    return [acc for _, acc in carry]


def _mla_attn_kernel(*refs, n_src, lens, tk, unroll):
    q_ref = refs[0]
    kv = refs[1:1 + 2 * n_src]
    o_ref = refs[1 + 2 * n_src]
    streams = [(q_ref[:, hh * HEAD_PAD:(hh + 1) * HEAD_PAD], hh * HEAD_PAD, hh * HEAD_PAD) for hh in range(2)]
    sources = [(kv[2 * s], kv[2 * s + 1], lens[s], min(tk, lens[s])) for s in range(n_src)]
    accs = _flash(streams, sources, HEAD_PAD, unroll)
    outs = [acc[:, :MLA_DV] / acc[:, MLA_DV:MLA_DV + 1] for acc in accs]
    o_ref[...] = jnp.concatenate(outs, axis=1).astype(BF16)


def _diff_attn_kernel(*refs, n_src, lens, tk, unroll, lam_init):
    q_ref, ll_ref, sub_ref = refs[:3]
    kv = refs[3:3 + 2 * n_src]
    o_ref = refs[3 + 2 * n_src]
    ll = ll_ref[...]
    lam = (jnp.exp(jnp.sum(ll[0:1, :] * ll[1:2, :], axis=1, keepdims=True))
           - jnp.exp(jnp.sum(ll[2:3, :] * ll[3:4, :], axis=1, keepdims=True)) + lam_init)
    q = q_ref[...]
    lane = lax.broadcasted_iota(jnp.int32, q.shape, 1)
    vw = 2 * HEAD_PAD
    dv = 2 * DIFF_D
    streams = [(jnp.where((lane // DIFF_D) == mp, q, jnp.zeros_like(q)), 0, 0) for mp in range(2)]
    sources = [(kv[2 * s], kv[2 * s + 1], lens[s], min(tk, lens[s])) for s in range(n_src)]
    accs = _flash(streams, sources, vw, unroll)
    o = [acc[:, :dv] / acc[:, dv:dv + 1] for acc in accs]
    od = o[0] - lam * o[1]
    o_ref[...] = (_rms(od, sub_ref[...]) * (1.0 - lam_init)).astype(BF16)


def _attention(kind, q, k, v, *, B, S, C, ctx_queries, extra=None, lam_init=None):
    N = B * S
    tk = ATTN_TK[kind]
    unroll = max(S // tk, 1)
    if kind == "mla":
        steps, qw, kw, vw, ow = MLA_HEADS // 2, 2 * HEAD_PAD, 2 * HEAD_PAD, 2 * HEAD_PAD, 2 * MLA_DV
    else:
        steps, qw, kw, vw, ow = DIFF_HEADS, HEAD_PAD, HEAD_PAD, 2 * HEAD_PAD, 2 * DIFF_D
    cblk = N // C
    if ctx_queries:
        tq, nq, rows_out = C, 1, B * C
        q_spec = pl.BlockSpec((tq, qw), lambda b, h, i: (cblk + b, h))
        kv_specs = [pl.BlockSpec((C, kw), lambda b, h, i: (cblk + b, h)),
                    pl.BlockSpec((C, vw), lambda b, h, i: (cblk + b, h))]
        lens = (C,)
    else:
        tq = ATTN_TQ
        nq, rows_out = S // tq, N
        q_spec = pl.BlockSpec((tq, qw), lambda b, h, i: (b * nq + i, h))
        kv_specs = [pl.BlockSpec((S, kw), lambda b, h, i: (b, h)),
                    pl.BlockSpec((S, vw), lambda b, h, i: (b, h)),
                    pl.BlockSpec((C, kw), lambda b, h, i: (cblk + b, h)),
                    pl.BlockSpec((C, vw), lambda b, h, i: (cblk + b, h))]
        lens = (S, C)
    n_src = len(lens)
    kv_args = [k, v] * n_src
    o_spec = pl.BlockSpec((tq, ow), lambda b, h, i: (b * nq + i, h))
    if kind == "mla":
        body = functools.partial(_mla_attn_kernel, n_src=n_src, lens=lens, tk=tk, unroll=unroll)
        args, specs = [q], [q_spec]
    else:
        ll, sub = extra
        body = functools.partial(_diff_attn_kernel, n_src=n_src, lens=lens, tk=tk, unroll=unroll, lam_init=lam_init)
        args = [q, ll, sub]
        specs = [q_spec, pl.BlockSpec(ll.shape, lambda b, h, i: (0, 0)), pl.BlockSpec(sub.shape, lambda b, h, i: (0, 0))]
    return pl.pallas_call(
        body,
        grid=(B, steps, nq),
        in_specs=specs + kv_specs,
        out_specs=o_spec,
        out_shape=jax.ShapeDtypeStruct((rows_out, steps * ow), BF16),
        compiler_params=_params("parallel", "parallel", "parallel"),
        name=f"{kind}_attn_{'ctx' if ctx_queries else 'lat'}",
    )(*args, *kv_args)


def _conv_kernel(main_ref, prev_ref, next_ref, w_ref, b_ref, g_ref, bb_ref, o_ref, z_ref, sh_ref, *, ts, nt):
    i = pl.program_id(1)

    def glu(blk):
        a = blk[:, :CONV_W].astype(F32)
        g = blk[:, CONV_W:].astype(F32)
        return a * jax.nn.sigmoid(g)

    H = CONV_HALO
    z_ref[0:H, :] = glu(prev_ref[...]) * (i > 0).astype(F32)
    z_ref[H:H + ts, :] = glu(main_ref[...])
    z_ref[H + ts:2 * H + ts, :] = glu(next_ref[...]) * (i < nt - 1).astype(F32)
    span = sh_ref.shape[1]
    for s in range(1, SUBLANES):
        sh_ref[s] = z_ref[s:s + span, :]
    rc = 32
    first = H - CONV_K // 2
    for r in range(ts // rc):
        acc = jnp.broadcast_to(b_ref[...], (rc, CONV_W))
        for k in range(CONV_K):
            s, a = (k + first) % SUBLANES, (k + first) // SUBLANES * SUBLANES
            src = z_ref[r * rc + a:r * rc + a + rc, :] if s == 0 else sh_ref[s, r * rc + a:r * rc + a + rc, :]
            acc = acc + w_ref[k:k + 1, :] * src
        mu = jnp.mean(acc, axis=-1, keepdims=True)
        d = acc - mu
        var = jnp.mean(d * d, axis=-1, keepdims=True)
        y = d * lax.rsqrt(var + EPS) * g_ref[...] + bb_ref[...]
        o_ref[r * rc:(r + 1) * rc, :] = (y * jax.nn.sigmoid(y)).astype(BF16)


def _conv(cols, row_off, nb, L, conv_w, conv_b, ln_g, ln_b):
    R = cols.shape[0]
    ts = min(ROW_TILE, L)
    nt = L // ts
    H = CONV_HALO
    hb = ts // H
    base = lambda b, i: (row_off + b * L + i * ts) // H
    vec = lambda a: pl.BlockSpec(a.shape, lambda b, i: (0, 0))
    cb, g, bb = conv_b.reshape(1, CONV_W), ln_g.reshape(1, CONV_W), ln_b.reshape(1, CONV_W)
    return pl.pallas_call(
        functools.partial(_conv_kernel, ts=ts, nt=nt),
        grid=(nb, nt),
        in_specs=[pl.BlockSpec((ts, CONV_COLS), lambda b, i: ((row_off + b * L) // ts + i, 0)),
                  pl.BlockSpec((H, CONV_COLS), lambda b, i: (jnp.maximum(base(b, i) - 1, 0), 0)),
                  pl.BlockSpec((H, CONV_COLS), lambda b, i: (jnp.minimum(base(b, i) + hb, R // H - 1), 0)),
                  vec(conv_w), vec(cb), vec(g), vec(bb)],
        out_specs=pl.BlockSpec((ts, CONV_W), lambda b, i: (b * nt + i, 0)),
        out_shape=jax.ShapeDtypeStruct((nb * L, CONV_W), BF16),
        scratch_shapes=[pltpu.VMEM((ts + 2 * H, CONV_W), F32),
                        pltpu.VMEM((SUBLANES, ts + 2 * H - SUBLANES, CONV_W), F32)],
        compiler_params=_params("parallel", "parallel"),
        name=f"conv_{L}",
    )(cols, cols, cols, conv_w, cb, g, bb)


def _dft_channel_matrix():
    j = jnp.arange(FNET_GROUP_W, dtype=jnp.int32)
    ang = (2.0 * math.pi / FNET_GROUP_W) * ((j[:, None] * j[None, :]) % FNET_GROUP_W).astype(F32)
    eye = jnp.eye(FNET_GROUPS, dtype=F32)
    scale = FNET_GROUP_W ** -0.5
    return jnp.concatenate([jnp.kron(eye, jnp.cos(ang)), jnp.kron(eye, jnp.sin(ang))], axis=1) * scale


def _dft_position_matrix(L):
    n = jnp.arange(L, dtype=jnp.int32)
    ang = (2.0 * math.pi / L) * ((n[:, None] * n[None, :]) % L).astype(F32)
    return jnp.concatenate([jnp.cos(ang), -jnp.sin(ang)], axis=1) * (L ** -0.5)


def _fnet_positions(y, a, row_off, nb, L):
    tm = min(ROW_TILE, L)
    tk = min(2048, L)
    kb = L // tk
    off = row_off // tk
    return _matmul(a, y, out_shape=(nb * L, FNET_COLS), out_dtype=F32, tm=tm, tn=FNET_COLS, tk=tk,
                   grid=(nb, L // tm, 1, 2 * kb),
                   a_map=lambda b, i, u, k: (i, k),
                   w_map=lambda b, i, u, k: (off + b * kb + k % kb, k // kb),
                   o_map=lambda b, i, u, k: (b * (L // tm) + i, 0),
                   name=f"fnet_pos_{L}")


FFT_MINOR = 128
FFT_GROUP = 8
FFT_COLS = 8


def _fft_major_matrix(n1):
    k = jnp.arange(n1, dtype=jnp.int32)
    ang = (2.0 * math.pi / n1) * ((k[:, None] * k[None, :]) % n1).astype(F32)
    return jnp.concatenate([jnp.cos(ang), jnp.sin(ang)], axis=0).astype(BF16)


def _fft_minor_matrices(L):
    n2c, grp = FFT_MINOR, FFT_GROUP
    n1 = L // n2c
    g = jnp.arange(n1 // grp, dtype=jnp.int32)[:, None, None, None, None]
    k2 = jnp.arange(n2c, dtype=jnp.int32)[None, :, None, None, None]
    j = jnp.arange(grp, dtype=jnp.int32)[None, None, :, None, None]
    jp = jnp.arange(grp, dtype=jnp.int32)[None, None, None, :, None]
    n2 = jnp.arange(n2c, dtype=jnp.int32)[None, None, None, None, :]
    phase = (n2 * k2 * n1 + n2 * (grp * g + j)) % L
    ang = (2.0 * math.pi / L) * phase.astype(F32)
    keep = (j == jp).astype(F32) * (L ** -0.5)
    shape = (n1 // grp, n2c * grp, grp * n2c)
    return ((jnp.cos(ang) * keep).reshape(shape).astype(BF16), (jnp.sin(ang) * keep).reshape(shape).astype(BF16))


def _fft_major_kernel(cs_ref, y_ref, a_ref, *, n1):
    pq = jnp.dot(cs_ref[...], y_ref[...], preferred_element_type=F32)
    p, q = pq[:n1], pq[n1:]
    w = FNET_COLS
    for t in range(FFT_COLS):
        re, im = slice(2 * t * w, (2 * t + 1) * w), slice((2 * t + 1) * w, (2 * t + 2) * w)
        a_ref[:, t, :] = jnp.concatenate([p[:, re] - q[:, im], -(p[:, im] + q[:, re])], axis=1)


def _fft_minor_kernel(mc_ref, ms_ref, a_ref, o_ref):
    w = FNET_COLS
    out = (jnp.dot(mc_ref[...], a_ref[:, :w].astype(BF16), preferred_element_type=F32)
           + jnp.dot(ms_ref[...], a_ref[:, w:].astype(BF16), preferred_element_type=F32))
    o_ref[...] = out.reshape(o_ref.shape)


def _fnet_positions_fft(y, nb, L):
    n2c, grp = FFT_MINOR, FFT_GROUP
    n1 = L // n2c
    w2 = 2 * FNET_COLS
    cs = _fft_major_matrix(n1)
    mc, ms = _fft_minor_matrices(L)
    flat = y[:nb * L].reshape(nb * n1, n2c * w2)
    tc = FFT_COLS * w2
    a = pl.pallas_call(
        functools.partial(_fft_major_kernel, n1=n1),
        grid=(nb, n2c // FFT_COLS),
        in_specs=[pl.BlockSpec((2 * n1, n1), lambda b, t: (0, 0)), pl.BlockSpec((n1, tc), lambda b, t: (b, t))],
        out_specs=pl.BlockSpec((n1, FFT_COLS, w2), lambda b, t: (b, t, 0)),
        out_shape=jax.ShapeDtypeStruct((nb * n1, n2c, w2), F32),
        compiler_params=_params("arbitrary", "arbitrary"),
        name="fft_major",
    )(cs, flat)
    a = a.reshape(nb * L, w2)
    rows = grp * n2c
    ng = n1 // grp
    out = pl.pallas_call(
        _fft_minor_kernel,
        grid=(ng, nb),
        in_specs=[pl.BlockSpec((None, rows, rows), lambda g, b: (g, 0, 0)),
                  pl.BlockSpec((None, rows, rows), lambda g, b: (g, 0, 0)),
                  pl.BlockSpec((rows, w2), lambda g, b: (b * ng + g, 0))],
        out_specs=pl.BlockSpec((None, n2c, grp, FNET_COLS), lambda g, b: (b, 0, g, 0)),
        out_shape=jax.ShapeDtypeStruct((nb, n2c, n1, FNET_COLS), F32),
        compiler_params=_params("arbitrary", "arbitrary"),
        name="fft_minor",
    )(mc, ms, a)
    return out.reshape(nb * L, FNET_COLS)


def _merge_kernel(*refs, lat_tiles):
    b_refs = refs[:2 * N_BRANCH]
    g_ref, wb_ref, wo_ref, o_ref = refs[2 * N_BRANCH:]
    D = D_MODEL
    is_ctx = pl.program_id(0) >= lat_tiles
    m = None
    for n in range(N_BRANCH):
        lat_ref, ctx_ref = b_refs[2 * n], b_refs[2 * n + 1]
        b = jnp.where(is_ctx, ctx_ref[...].astype(BF16), lat_ref[...].astype(BF16))
        proj = jnp.dot(b, wb_ref[n], preferred_element_type=F32)
        t = jax.nn.sigmoid(g_ref[:, n * D:(n + 1) * D].astype(F32)) * proj
        m = t if m is None else m + t
    o_ref[...] = jnp.dot(m.astype(BF16), wo_ref[...], preferred_element_type=F32)


def _merge(branches, gates, w_branch, w_out):
    R = gates.shape[0]
    D = D_MODEL
    tm = MERGE_TILE
    lat_tiles = branches[0][0].shape[0] // tm
    ctx_tiles = branches[0][1].shape[0] // tm
    lat_spec = pl.BlockSpec((tm, BRANCH_W), lambda i: (jnp.minimum(i, lat_tiles - 1), 0))
    ctx_spec = pl.BlockSpec((tm, BRANCH_W), lambda i: (jnp.clip(i - lat_tiles, 0, ctx_tiles - 1), 0))
    specs = [lat_spec, ctx_spec] * N_BRANCH
    branches = [a for pair in branches for a in pair]
    specs += [pl.BlockSpec((tm, N_BRANCH * D), lambda i: (i, 0)),
              pl.BlockSpec((N_BRANCH, BRANCH_W, D), lambda i: (0, 0, 0)),
              pl.BlockSpec((D, D), lambda i: (0, 0))]
    return pl.pallas_call(
        functools.partial(_merge_kernel, lat_tiles=lat_tiles),
        grid=(R // tm,),
        in_specs=specs,
        out_specs=pl.BlockSpec((tm, D), lambda i: (i, 0)),
        out_shape=jax.ShapeDtypeStruct((R, D), F32),
        compiler_params=_params("arbitrary"),
        name="merge",
    )(*branches, gates, w_branch, w_out)


def _moe_plan(rec, counts, R):
    T = MOE_TILE
    cnt = counts[:, 0].astype(jnp.int32)
    padded = ((cnt + T - 1) // T) * T
    ends = jnp.cumsum(padded)
    base = ends - padded
    n_tiles = (2 * R) // T + N_EXPERTS
    n_used = (ends[-1] // T).astype(jnp.int32)
    tile = jnp.minimum(jnp.arange(n_tiles, dtype=jnp.int32), n_used - 1)
    tile_expert = jnp.sum((ends // T)[None, :] <= tile[:, None], axis=1).astype(jnp.int32)
    tile_expert = jnp.minimum(tile_expert, N_EXPERTS - 1)
    e_ab = rec[0:2].astype(jnp.int32)
    r_ab = rec[2:4].astype(jnp.int32)
    base_ab = jnp.zeros_like(e_ab)
    for e in range(N_EXPERTS):
        base_ab = jnp.where(e_ab == e, base[e], base_ab)
    pos = (base_ab + r_ab - 1).T.reshape(2 * R)
    wgt = rec[4:6].T
    tail = jnp.concatenate([ends - T, (cnt > 0).astype(jnp.int32), n_used.reshape(1)]).astype(jnp.int32)
    return pos, wgt, tile_expert, n_used.reshape(1), tail


def _row_copy(src, src_row, dst, dst_row, sem):
    return pltpu.make_async_copy(src.at[pl.ds(src_row, 1)], dst.at[pl.ds(dst_row, 1)], sem)


def _dispatch_kernel(pos_ref, tail_ref, h_ref, buf_ref, zero_ref, sem, zsem, *, tm, n_tiles):
    i = pl.program_id(0)

    @pl.when(i == 0)
    def _():
        zero_ref[...] = jnp.zeros_like(zero_ref)

        def tail_copy(e):
            row = pl.multiple_of(tail_ref[e], MOE_TILE)
            return pltpu.make_async_copy(zero_ref, buf_ref.at[pl.ds(row, MOE_TILE)], zsem)

        for e in range(N_EXPERTS):
            @pl.when(tail_ref[N_EXPERTS + e] > 0)
            def _():
                tail_copy(e).start()
        for e in range(N_EXPERTS):
            @pl.when(tail_ref[N_EXPERTS + e] > 0)
            def _():
                tail_copy(e).wait()

        def unused_copy(t):
            return pltpu.make_async_copy(zero_ref, buf_ref.at[pl.ds(pl.multiple_of(t * MOE_TILE, MOE_TILE), MOE_TILE)],
                                         zsem)

        def clear(t, carry):
            unused_copy(t).start()
            unused_copy(t).wait()
            return carry

        lax.fori_loop(tail_ref[2 * N_EXPERTS], n_tiles, clear, 0)

    for t in range(tm):
        for slot in range(2):
            _row_copy(h_ref, t, buf_ref, pos_ref[(i * tm + t) * 2 + slot], sem).start()
    for slot in range(2):
        pltpu.make_async_copy(h_ref, buf_ref.at[pl.ds(0, tm)], sem).wait()


def _moe_dispatch(h, pos, tail, n_rows):
    R, D = h.shape
    tm = ROW_TILE
    return pl.pallas_call(
        functools.partial(_dispatch_kernel, tm=tm, n_tiles=n_rows // MOE_TILE),
        grid_spec=pltpu.PrefetchScalarGridSpec(
            num_scalar_prefetch=2,
            grid=(R // tm,),
            in_specs=[pl.BlockSpec((tm, D), lambda i, pos, tail: (i, 0))],
            out_specs=pl.BlockSpec(memory_space=pl.ANY),
            scratch_shapes=[pltpu.VMEM((MOE_TILE, D), F32), pltpu.SemaphoreType.DMA, pltpu.SemaphoreType.DMA],
        ),
        out_shape=jax.ShapeDtypeStruct((n_rows, D), F32),
        compiler_params=_params("arbitrary"),
        name="moe_dispatch",
    )(pos, tail, h)


def _experts_kernel(te_ref, nu_ref, x_ref, w1_ref, w3_ref, w2_ref, o_ref):
    @pl.when(pl.program_id(0) >= nu_ref[0])
    def _():
        o_ref[...] = jnp.zeros_like(o_ref)

    @pl.when(pl.program_id(0) < nu_ref[0])
    def _():
        x = x_ref[...].astype(BF16)
        a = jnp.dot(x, w1_ref[...], preferred_element_type=F32)
        b = jnp.dot(x, w3_ref[...], preferred_element_type=F32)
        hid = (a * jax.nn.sigmoid(a) * b).astype(BF16)
        o_ref[...] = jnp.dot(hid, w2_ref[...], preferred_element_type=F32)


def _moe_experts(xs, tile_expert, n_used, w1, w3, w2):
    P, D = xs.shape
    T = MOE_TILE
    row = pl.BlockSpec((T, D), lambda i, te, nu: (i, 0))
    return pl.pallas_call(
        _experts_kernel,
        grid_spec=pltpu.PrefetchScalarGridSpec(
            num_scalar_prefetch=2,
            grid=(P // T,),
            in_specs=[row,
                      pl.BlockSpec((None, D, D_FF), lambda i, te, nu: (te[i], 0, 0)),
                      pl.BlockSpec((None, D, D_FF), lambda i, te, nu: (te[i], 0, 0)),
                      pl.BlockSpec((None, D_FF, D), lambda i, te, nu: (te[i], 0, 0))],
            out_specs=row,
        ),
        out_shape=jax.ShapeDtypeStruct((P, D), F32),
        compiler_params=_params("arbitrary"),
        name="moe_experts",
    )(tile_expert, n_used, xs, w1, w3, w2)


def _combine_norm_kernel(*refs, tc, n_tiles, alpha, emit_h):
    refs = list(refs)
    pos_ref, x_ref, w_ref, g_ref, lng_ref, lnb_ref = refs[:6]
    refs = refs[6:]
    if emit_h:
        sc_ref, sh_ref = refs[:2]
        refs = refs[2:]
    y_ref, xo_ref = refs[:2]
    refs = refs[2:]
    if emit_h:
        h_ref = refs.pop(0)
    bufs = (refs[0:2], refs[2:4])
    sem = refs[4]
    i = pl.program_id(0)

    def issue(tile, slot, t):
        p = (tile * tc + t) * 2
        _row_copy(y_ref, pos_ref[p], bufs[slot][0], t, sem.at[slot]).start()
        _row_copy(y_ref, pos_ref[p + 1], bufs[slot][1], t, sem.at[slot]).start()

    def wait(slot):
        for buf in bufs[slot]:
            pltpu.make_async_copy(y_ref.at[pl.ds(0, tc)], buf, sem.at[slot]).wait()

    @pl.when(i == 0)
    def _():
        lax.fori_loop(0, tc, lambda t, c: (issue(0, 0, t), c)[1], 0, unroll=8)

    nxt = jnp.minimum(i + 1, n_tiles - 1)
    for slot in range(2):
        @pl.when(i % 2 == slot)
        def _(slot=slot):
            wait(slot)
            for t in range(tc):
                issue(nxt, 1 - slot, t)
            ya_ref, yb_ref = bufs[slot]
            w = w_ref[...]
            f = w[:, 0:1] * ya_ref[...] + w[:, 1:2] * yb_ref[...]
            xn = _standardize(alpha * x_ref[...] + g_ref[...] * f) * lng_ref[...] + lnb_ref[...]
            xo_ref[...] = xn
            if emit_h:
                h_ref[...] = (xn * (1.0 + sc_ref[...]) + sh_ref[...]).astype(BF16)

            @pl.when(i == n_tiles - 1)
            def _():
                wait(1 - slot)


def _moe_combine_norm(x, ys, pos, wgt, *, seg_of_row, alpha, mod_g, k_gate, ln_g, ln_b, mod_h=None, k_scale=None,
                      k_shift=None, name):
    R, D = x.shape
    tc = COMBINE_TILE
    emit_h = mod_h is not None
    row = pl.BlockSpec((tc, D), lambda i, pos: (i, 0))

    def mod_spec(k):
        return pl.BlockSpec((None, 1, D), lambda i, pos: (seg_of_row(i * tc) * 6 + k, 0, 0))

    vec = pl.BlockSpec((1, D), lambda i, pos: (0, 0))
    args = [x, wgt, mod_g, ln_g.reshape(1, D), ln_b.reshape(1, D)]
    specs = [row, pl.BlockSpec((tc, 2), lambda i, pos: (i, 0)), mod_spec(k_gate), vec, vec]
    if emit_h:
        args += [mod_h, mod_h]
        specs += [mod_spec(k_scale), mod_spec(k_shift)]
    args.append(ys)
    specs.append(pl.BlockSpec(memory_space=pl.ANY))
    out_shape = [jax.ShapeDtypeStruct((R, D), F32)]
    out_specs = [row]
    if emit_h:
        out_shape.append(jax.ShapeDtypeStruct((R, D), BF16))
        out_specs.append(row)
    return pl.pallas_call(
        functools.partial(_combine_norm_kernel, tc=tc, n_tiles=R // tc, alpha=alpha, emit_h=emit_h),
        grid_spec=pltpu.PrefetchScalarGridSpec(
            num_scalar_prefetch=1,
            grid=(R // tc,),
            in_specs=specs,
            out_specs=out_specs,
            scratch_shapes=[pltpu.VMEM((tc, D), F32)] * 4 + [pltpu.SemaphoreType.DMA((2,))],
        ),
        out_shape=out_shape,
        compiler_params=_params("arbitrary"),
        name=name,
    )(pos, *args)


def _pad_heads(w, heads, width, lo, hi):
    lead = w.shape[:-1]
    w = w.reshape(lead + (heads, width))[..., lo:hi]
    w = jnp.pad(w, ((0, 0),) * (len(lead) + 1) + ((0, HEAD_PAD - (hi - lo)),))
    return w.reshape(lead + (heads * HEAD_PAD,))


def kernel(x, c, ctx, c_ctx, w_ada, b_ada, w_in, mla_q_norm, mla_w_uq, mla_kv_norm, mla_w_ukv, diff_lq1, diff_lk1,
           diff_lq2, diff_lk2, diff_subln, conv_w, conv_b, conv_ln_g, conv_ln_b, w_branch, w_out, ln1_g, ln1_b,
           ln2_g, ln2_b, router_w, router_bias, exp_w1, exp_w3, exp_w2):
    B, S, D = x.shape
    C = ctx.shape[1]
    depth = w_in.shape[0]
    N, NC = B * S, B * C
    tm = ROW_TILE
    assert D == D_MODEL and S % tm == 0 and NC % tm == 0 and S % C == 0 and S % GRID_W == 0

    R = N + NC

    def seg_of_row(r):
        return jnp.minimum(r // S, B)

    def seg(i):
        return seg_of_row(i * tm)

    def pos_block(i):
        return jnp.where(i < N // tm, i % (S // tm), S // tm)

    rows = S // GRID_W
    tabs_mq = _rope_tables(rows, MLA_DR, MLA_DN, HEAD_PAD, MLA_HEADS, tm)
    tabs_mk = _rope_tables(rows, MLA_DR, 0, LANES, 1, tm)
    tabs_d = _rope_tables(rows, DIFF_D, 0, DIFF_D, 2 * DIFF_HEADS, tm)

    nseg = 8
    cond = jnp.zeros((nseg, D), F32).at[:B].set(c).at[B].set(c_ctx)
    mod_all = _ada(cond, w_ada, b_ada).reshape(depth, nseg * 6, 1, D)

    dft_ch = _dft_channel_matrix().astype(BF16)
    dft_ctx = _dft_position_matrix(C).astype(BF16)
    if S % (FFT_MINOR * FFT_GROUP) == 0:
        fnet_lat = lambda y: _fnet_positions_fft(y, B, S)
    else:
        dft_lat = _dft_position_matrix(S).astype(BF16)
        fnet_lat = lambda y: _fnet_positions(y, dft_lat, 0, B, S)
    rw_t = router_w.T
    rw_hi = rw_t.astype(BF16)
    rw_lo = (rw_t - rw_hi.astype(F32)).astype(BF16)
    router = (rw_hi, rw_lo, router_bias.reshape(N_EXPERTS, 1))

    xs = jnp.concatenate([x.reshape(N, D), ctx.reshape(NC, D)], axis=0)
    alpha = (2.0 * depth) ** 0.25
    xs, h = _norm(xs, seg=seg, alpha=alpha, first=True, mod_h=mod_all[0], k_scale=1, k_shift=0, name="norm_in")

    w_mla = jnp.pad(w_in[:, :, :OFF_DIFF], ((0, 0), (0, 0), (0, MLA_COLS_PAD - MLA_COLS))).astype(BF16)
    w_diff = w_in[:, :, OFF_DIFF:OFF_CONV].astype(BF16)
    w_conv = w_in[:, :, OFF_CONV:OFF_FNET].astype(BF16)
    w_fnet = w_in[:, :, OFF_FNET:OFF_GATE].astype(BF16)
    w_gate = w_in[:, :, OFF_GATE:].astype(BF16)
    wq = _pad_heads(mla_w_uq, MLA_HEADS, MLA_DN + MLA_DR, 0, MLA_DN + MLA_DR).astype(BF16)
    wk = _pad_heads(mla_w_ukv, MLA_HEADS, MLA_DN + MLA_DV, 0, MLA_DN).astype(BF16)
    wv = _pad_heads(mla_w_ukv, MLA_HEADS, MLA_DN + MLA_DV, MLA_DN, MLA_DN + MLA_DV).astype(BF16)

    for l in range(depth):
        mod = mod_all[l]
        cm = _project(h, w_mla, F32, "proj_mla", MLA_COLS_PAD, layer=l)
        cd = _project(h, w_diff, BF16, "proj_diff", DIFF_COLS, layer=l)
        cc = _project(h, w_conv, BF16, "proj_conv", CONV_COLS, layer=l)
        cf = _project(h, w_fnet, BF16, "proj_fnet", FNET_COLS, layer=l)
        gates = _project(h, w_gate, BF16, "proj_gate", 2048, layer=l)

        mq, mk, mv = _mla_prep(cm, mla_q_norm[l].reshape(1, -1), mla_kv_norm[l].reshape(1, -1), wq[l], wk[l], wv[l],
                               tabs_mq, tabs_mk, pos_block)
        o_mla = (_attention("mla", mq, mk, mv, B=B, S=S, C=C, ctx_queries=False),
                 _attention("mla", mq, mk, mv, B=B, S=S, C=C, ctx_queries=True))

        lam_init = 0.8 - 0.6 * math.exp(-0.3 * l)
        ll = jnp.stack([diff_lq1[l], diff_lk1[l], diff_lq2[l], diff_lk2[l]], axis=0)
        sub = diff_subln[l].reshape(1, -1)
        dq, dk, dv = _diff_prep(cd, tabs_d, pos_block)
        o_diff = (_attention("diff", dq, dk, dv, B=B, S=S, C=C, ctx_queries=False, extra=(ll, sub), lam_init=lam_init),
                  _attention("diff", dq, dk, dv, B=B, S=S, C=C, ctx_queries=True, extra=(ll, sub), lam_init=lam_init))

        o_conv = (_conv(cc, 0, B, S, conv_w[l], conv_b[l], conv_ln_g[l], conv_ln_b[l]),
                  _conv(cc, N, B, C, conv_w[l], conv_b[l], conv_ln_g[l], conv_ln_b[l]))
        y = _project(cf, dft_ch, BF16, "fnet_ch", 2 * FNET_COLS)
        o_fnet = (fnet_lat(y), _fnet_positions(y, dft_ctx, N, B, C))

        mix = _merge([o_mla, o_diff, o_conv, o_fnet], gates, w_branch[l].astype(BF16), w_out[l].astype(BF16))
        xs, h2, rec, counts = _norm(xs, seg=seg, alpha=alpha, m=mix, mod_g=mod, k_gate=2, ln_g=ln1_g[l],
                                    ln_b=ln1_b[l], mod_h=mod, k_scale=4, k_shift=3, router=router, name="norm1")
        pos, wgt, tile_expert, n_used, tail = _moe_plan(rec, counts, R)
        sorted_rows = _moe_dispatch(h2, pos, tail, tile_expert.shape[0] * MOE_TILE)
        ys = _moe_experts(sorted_rows, tile_expert, n_used, exp_w1[l].astype(BF16), exp_w3[l].astype(BF16),
                          exp_w2[l].astype(BF16))
        last = l + 1 == depth
        out = _moe_combine_norm(xs, ys, pos, wgt, seg_of_row=seg_of_row, alpha=alpha, mod_g=mod, k_gate=5,
                                ln_g=ln2_g[l], ln_b=ln2_b[l], mod_h=None if last else mod_all[l + 1],
                                k_scale=1, k_shift=0, name="norm_out" if last else "norm2")
        if last:
            (xs,) = out
        else:
            xs, h = out
    return xs[:N].reshape(B, S, D)
```
